```python
import math
import jax, jax.numpy as jnp
from jax import lax
import numpy as np

D_MODEL = 1024
BATCH = 4
SEQ = 4096
DEPTH = 4

GRID_W = 64
N_HEADS = 8
N_KV_HEADS = 2
HEAD_DIM = 128
Q_BLOCK = 128
ROPE_THETA = 10000.0
C_CONV = D_MODEL
CONV_K = 31
N_EXPERTS = 16
D_FF_EXPERT = 2 * D_MODEL
EC_FACTOR = 2
LN_EPS = 1e-5
RMS_EPS = 1e-6
ALPHA = (2.0 * DEPTH) ** 0.25
BETA = (8.0 * DEPTH) ** -0.25

Q_W = N_HEADS * HEAD_DIM
KV_W = N_KV_HEADS * HEAD_DIM
OFF_CONV = 0
OFF_Q = OFF_CONV + 2 * C_CONV
OFF_K = OFF_Q + Q_W
OFF_V = OFF_K + KV_W
OFF_GC = OFF_V + KV_W
OFF_GA = OFF_GC + D_MODEL
N_IN = OFF_GA + D_MODEL

kernel_name = "hybrid_conv_axialgqa_ecmoe_deepnorm"


def layer_norm(x, g, b):
    xf = x.astype(jnp.float32)
    mu = jnp.mean(xf, axis=-1, keepdims=True)
    xc = xf - mu
    var = jnp.mean(xc * xc, axis=-1, keepdims=True)
    return (xc * lax.rsqrt(var + LN_EPS) * g + b).astype(x.dtype)


def rms_norm(x, g):
    xf = x.astype(jnp.float32)
    ms = jnp.mean(xf * xf, axis=-1, keepdims=True)
    return (xf * lax.rsqrt(ms + RMS_EPS) * g).astype(x.dtype)


def axial_rope_tables(seq):
    rows = seq // GRID_W
    row = jnp.repeat(jnp.arange(rows, dtype=jnp.int32), GRID_W).astype(jnp.float32)
    col = jnp.tile(jnp.arange(GRID_W, dtype=jnp.int32), rows).astype(jnp.float32)
    axis_dim = HEAD_DIM // 2
    freqs = 1.0 / (ROPE_THETA ** (jnp.arange(0, axis_dim, 2, dtype=jnp.float32) / axis_dim))
    ang = jnp.concatenate([row[:, None] * freqs[None], col[:, None] * freqs[None]], axis=-1)
    return jnp.cos(ang), jnp.sin(ang)


def apply_rope(x, cos, sin):
    xf = x.astype(jnp.float32).reshape(x.shape[:-1] + (HEAD_DIM // 2, 2))
    x0, x1 = xf[..., 0], xf[..., 1]
    c = cos[None, :, None, :]
    s = sin[None, :, None, :]
    out = jnp.stack([x0 * c - x1 * s, x0 * s + x1 * c], axis=-1)
    return out.reshape(x.shape).astype(x.dtype)


def gqa_attention(q, k, v):
    b, s = q.shape[0], q.shape[1]
    n_blk = s // Q_BLOCK
    grp = N_HEADS // N_KV_HEADS
    scale = 1.0 / math.sqrt(HEAD_DIM)
    qb = q.reshape(b, n_blk, Q_BLOCK, N_KV_HEADS, grp, HEAD_DIM).transpose(1, 0, 2, 3, 4, 5)

    def block(qi):
        sc = jnp.einsum('bqkgd,bskd->bkgqs', qi, k, preferred_element_type=jnp.float32) * scale
        p = jax.nn.softmax(sc, axis=-1)
        return jnp.einsum('bkgqs,bskd->bqkgd', p.astype(v.dtype), v)

    o = lax.map(block, qb)
    return o.transpose(1, 0, 2, 3, 4, 5).reshape(b, s, N_HEADS * HEAD_DIM)


def conformer_conv(val, gate, dw, dw_b, ln_g, ln_b, pw_w, pw_b):
    u = val * jax.nn.sigmoid(gate)
    u = lax.conv_general_dilated(
        u, dw[:, None, :], window_strides=(1,),
        padding=[(CONV_K // 2, CONV_K // 2)],
        dimension_numbers=('NWC', 'WIO', 'NWC'),
        feature_group_count=C_CONV) + dw_b
    u = jax.nn.silu(layer_norm(u, ln_g, ln_b))
    return u @ pw_w + pw_b


def expert_choice_ffn(h, w_router, w_gate, w_up, w_down):
    b, s, d = h.shape
    cap = EC_FACTOR * s // N_EXPERTS
    logits = jnp.einsum('bsd,de->bse', h, w_router, preferred_element_type=jnp.float32)
    aff = jax.nn.softmax(logits, axis=-1)
    g, idx = lax.top_k(aff.transpose(0, 2, 1), cap)
    xe = jax.vmap(lambda hb, ib: hb[ib])(h, idx)
    hid = jax.nn.silu(jnp.einsum('becd,edf->becf', xe, w_gate)) * jnp.einsum('becd,edf->becf', xe, w_up)
    ye = jnp.einsum('becf,efd->becd', hid, w_down) * g[..., None].astype(h.dtype)
    out = jax.vmap(lambda ib, yb: jnp.zeros((s, d), yb.dtype).at[ib.reshape(-1)].add(yb.reshape(-1, d)))(idx, ye)
    return out


def setup_inputs(seed: int = 0) -> dict:
    key = jax.random.key(seed)
    ks = jax.random.split(key, 24)
    L, D = DEPTH, D_MODEL
    f32 = jnp.float32

    def nrm(k, shape, scale):
        return jax.random.normal(k, shape, f32) * scale

    col_scale = jnp.concatenate([
        jnp.ones((OFF_V,), f32), jnp.full((KV_W,), BETA, f32), jnp.ones((N_IN - OFF_GC,), f32)])
    w_in = nrm(ks[1], (L, D, N_IN), D ** -0.5) * col_scale
    return {
        "x": nrm(ks[0], (BATCH, SEQ, D), 1.0),
        "ln0_g": 1.0 + nrm(ks[2], (D,), 0.02),
        "ln0_b": nrm(ks[3], (D,), 0.02),
        "w_in": w_in,
        "b_in": nrm(ks[4], (L, N_IN), 0.02),
        "conv_dw": nrm(ks[5], (L, CONV_K, C_CONV), CONV_K ** -0.5),
        "conv_dw_b": nrm(ks[6], (L, C_CONV), 0.02),
        "conv_ln_g": 1.0 + nrm(ks[7], (L, C_CONV), 0.02),
        "conv_ln_b": nrm(ks[8], (L, C_CONV), 0.02),
        "conv_pw_w": nrm(ks[9], (L, C_CONV, D), BETA * C_CONV ** -0.5),
        "conv_pw_b": nrm(ks[10], (L, D), 0.02),
        "q_norm_g": 1.0 + nrm(ks[11], (L, HEAD_DIM), 0.02),
        "k_norm_g": 1.0 + nrm(ks[12], (L, HEAD_DIM), 0.02),
        "w_o": nrm(ks[13], (L, Q_W, D), BETA * Q_W ** -0.5),
        "w_out": nrm(ks[14], (L, D, D), BETA * D ** -0.5),
        "b_out": nrm(ks[15], (L, D), 0.02),
        "ln1_g": 1.0 + nrm(ks[16], (L, D), 0.02),
        "ln1_b": nrm(ks[17], (L, D), 0.02),
        "w_router": nrm(ks[18], (L, D, N_EXPERTS), D ** -0.5),
        "w_gate": nrm(ks[19], (L, N_EXPERTS, D, D_FF_EXPERT), D ** -0.5),
        "w_up": nrm(ks[20], (L, N_EXPERTS, D, D_FF_EXPERT), D ** -0.5),
        "w_down": nrm(ks[21], (L, N_EXPERTS, D_FF_EXPERT, D), BETA * D_FF_EXPERT ** -0.5),
        "ln2_g": 1.0 + nrm(ks[22], (L, D), 0.02),
        "ln2_b": nrm(ks[23], (L, D), 0.02),
    }


def reference(x, ln0_g, ln0_b, w_in, b_in, conv_dw, conv_dw_b, conv_ln_g, conv_ln_b,
              conv_pw_w, conv_pw_b, q_norm_g, k_norm_g, w_o, w_out, b_out, ln1_g, ln1_b,
              w_router, w_gate, w_up, w_down, ln2_g, ln2_b):
    b, s, _ = x.shape
    cos, sin = axial_rope_tables(s)
    x = layer_norm(x, ln0_g, ln0_b)
    for l in range(DEPTH):
        h = x
        z = h @ w_in[l] + b_in[l]
        y_c = conformer_conv(z[..., OFF_CONV:OFF_CONV + C_CONV], z[..., OFF_CONV + C_CONV:OFF_Q],
                             conv_dw[l], conv_dw_b[l], conv_ln_g[l], conv_ln_b[l],
                             conv_pw_w[l], conv_pw_b[l])
        q = z[..., OFF_Q:OFF_K].reshape(b, s, N_HEADS, HEAD_DIM)
        k = z[..., OFF_K:OFF_V].reshape(b, s, N_KV_HEADS, HEAD_DIM)
        v = z[..., OFF_V:OFF_GC].reshape(b, s, N_KV_HEADS, HEAD_DIM)
        q = apply_rope(rms_norm(q, q_norm_g[l]), cos, sin)
        k = apply_rope(rms_norm(k, k_norm_g[l]), cos, sin)
        y_a = gqa_attention(q, k, v) @ w_o[l]
        g_c = jax.nn.sigmoid(z[..., OFF_GC:OFF_GA])
        g_a = jax.nn.sigmoid(z[..., OFF_GA:N_IN])
        mix = (g_c * y_c + g_a * y_a) @ w_out[l] + b_out[l]
        x = layer_norm(ALPHA * x + mix, ln1_g[l], ln1_b[l])
        moe = expert_choice_ffn(x, w_router[l], w_gate[l], w_up[l], w_down[l])
        x = layer_norm(ALPHA * x + moe, ln2_g[l], ln2_b[l])
    return x
```

```python
import functools
import math

import jax
import jax.numpy as jnp
from jax import lax
from jax.experimental import pallas as pl
from jax.experimental.pallas import tpu as pltpu

F32 = jnp.float32
BF16 = jnp.bfloat16

D_MODEL = 1024
N_HEADS = 8
N_KV_HEADS = 2
HEAD_DIM = 128
GROUP = N_HEADS // N_KV_HEADS
GRID_W = 64
ROPE_THETA = 10000.0
C_CONV = D_MODEL
CONV_K = 31
CONV_HALO = 16
N_EXPERTS = 16
EC_FACTOR = 2
LN_EPS = 1e-5
RMS_EPS = 1e-6
Q_W = N_HEADS * HEAD_DIM
KV_W = N_KV_HEADS * HEAD_DIM

LANES = 128
VMEM_LIMIT = 56 * 1024 * 1024

TM_LN = 512
TM_PROJ = 512
TS_CONV = 256
RC_CONV = 32
TQ_ATTN = 256
TK_ATTN = 512
TM_MERGE = 256
TF_FFN = 512
TT_COMB = 256


def _params(*sem):
    return pltpu.CompilerParams(dimension_semantics=sem, vmem_limit_bytes=VMEM_LIMIT)


def _sigmoid(x):
    return 1.0 / (1.0 + jnp.exp(-x))


def _layer_norm(x, g, b):
    mu = jnp.mean(x, axis=-1, keepdims=True)
    xc = x - mu
    var = jnp.mean(xc * xc, axis=-1, keepdims=True)
    return xc * lax.rsqrt(var + LN_EPS) * g + b


def _dot(a, b):
    return jnp.dot(a, b, preferred_element_type=F32)


def _ln0_kernel(x_ref, g_ref, b_ref, xf_ref, xb_ref):
    y = _layer_norm(x_ref[...], g_ref[...], b_ref[...])
    xf_ref[...] = y
    xb_ref[...] = y.astype(BF16)


def _ln0(x2d, g, b):
    m, d = x2d.shape
    row = pl.BlockSpec((TM_LN, d), lambda i: (i, 0))
    vec = pl.BlockSpec((1, d), lambda i: (0, 0))
    return pl.pallas_call(
        _ln0_kernel,
        grid=(m // TM_LN,),
        in_specs=[row, vec, vec],
        out_specs=[row, row],
        out_shape=[jax.ShapeDtypeStruct((m, d), F32), jax.ShapeDtypeStruct((m, d), BF16)],
        compiler_params=_params("parallel"),
        name="ln0",
    )(x2d, g.reshape(1, d), b.reshape(1, d))


def _glu_kernel(h_ref, wv_ref, wg_ref, bv_ref, bg_ref, u_ref):
    h = h_ref[...]
    val = _dot(h, wv_ref[...]) + bv_ref[...]
    gate = _dot(h, wg_ref[...]) + bg_ref[...]
    u_ref[...] = val * _sigmoid(gate)


def _glu(hb, wv, wg, bv, bg):
    m, d = hb.shape
    c = wv.shape[1]
    row_in = pl.BlockSpec((TM_PROJ, d), lambda i: (i, 0))
    w = pl.BlockSpec((d, c), lambda i: (0, 0))
    vec = pl.BlockSpec((1, c), lambda i: (0, 0))
    return pl.pallas_call(
        _glu_kernel,
        grid=(m // TM_PROJ,),
        in_specs=[row_in, w, w, vec, vec],
        out_specs=pl.BlockSpec((TM_PROJ, c), lambda i: (i, 0)),
        out_shape=jax.ShapeDtypeStruct((m, c), F32),
        compiler_params=_params("parallel"),
        name="glu",
    )(hb, wv, wg, bv.reshape(1, c), bg.reshape(1, c))


def _norm_rope(x, g, cos, sin):
    ms = jnp.mean(x * x, axis=-1, keepdims=True)
    xn = x * lax.rsqrt(ms + RMS_EPS) * g
    return xn * cos + pltpu.roll(xn, HEAD_DIM // 2, 1) * sin


def _qkv_kernel(h_ref, wq_ref, wk_ref, wv_ref, bq_ref, bk_ref, bv_ref, gq_ref, gk_ref,
                cq_ref, sq_ref, ck_ref, sk_ref, q_ref, k_ref, v_ref):
    h = h_ref[...]
    zq = _dot(h, wq_ref[...]) + bq_ref[...]
    zk = _dot(h, wk_ref[...]) + bk_ref[...]
    zv = _dot(h, wv_ref[...]) + bv_ref[...]
    cq, sq, ck, sk = cq_ref[...], sq_ref[...], ck_ref[...], sk_ref[...]
    gq, gk = gq_ref[...], gk_ref[...]
    for hd in range(N_HEADS):
        sl = slice(hd * HEAD_DIM, (hd + 1) * HEAD_DIM)
        q_ref[:, sl] = _norm_rope(zq[:, sl], gq, cq, sq).astype(BF16)
    for hd in range(N_KV_HEADS):
        sl = slice(hd * HEAD_DIM, (hd + 1) * HEAD_DIM)
        k_ref[:, sl] = _norm_rope(zk[:, sl], gk, ck, sk).astype(BF16)
    v_ref[...] = zv.astype(BF16)


def _qkv(hb, wq, wk, wv, bq, bk, bv, gq, gk, cq, sq, ck, sk, seq):
    m, d = hb.shape
    tm = TM_PROJ
    per_seq = seq // tm
    row = lambda w: pl.BlockSpec((tm, w), lambda i: (i, 0))
    full = lambda a: pl.BlockSpec(a.shape, lambda i: (0, 0))
    tab = pl.BlockSpec((tm, HEAD_DIM), lambda i: (i % per_seq, 0))
    args = (hb, wq, wk, wv, bq.reshape(1, -1), bk.reshape(1, -1), bv.reshape(1, -1),
            gq.reshape(1, -1), gk.reshape(1, -1), cq, sq, ck, sk)
    in_specs = [row(d)] + [full(a) for a in args[1:9]] + [tab] * 4
    return pl.pallas_call(
        _qkv_kernel,
        grid=(m // tm,),
        in_specs=in_specs,
        out_specs=[row(Q_W), row(KV_W), row(KV_W)],
        out_shape=[jax.ShapeDtypeStruct((m, Q_W), BF16), jax.ShapeDtypeStruct((m, KV_W), BF16),
                   jax.ShapeDtypeStruct((m, KV_W), BF16)],
        compiler_params=_params("parallel"),
        name="qkv",
    )(*args)


def _conv_kernel(up_ref, uc_ref, un_ref, w_ref, wb_ref, g_ref, b_ref, o_ref, ext_ref):
    i = pl.program_id(1)
    n = pl.num_programs(1)
    ts = uc_ref.shape[1]
    ext_ref[0:CONV_HALO, :] = jnp.where(i > 0, up_ref[0], 0.0)
    ext_ref[CONV_HALO:CONV_HALO + ts, :] = uc_ref[0]
    ext_ref[CONV_HALO + ts:, :] = jnp.where(i < n - 1, un_ref[0], 0.0)
    g, b = g_ref[...], b_ref[...]
    first = CONV_HALO - CONV_K // 2
    for r0 in range(0, ts, RC_CONV):
        acc = jnp.zeros((RC_CONV, uc_ref.shape[2]), F32) + wb_ref[...]
        for k in range(CONV_K):
            acc = acc + w_ref[k:k + 1, :] * ext_ref[r0 + first + k:r0 + first + k + RC_CONV, :]
        y = _layer_norm(acc, g, b)
        o_ref[0, r0:r0 + RC_CONV, :] = (y * _sigmoid(y)).astype(BF16)


def _conv(u, dw, dw_b, ln_g, ln_b):
    bsz, seq, c = u.shape
    ts = TS_CONV
    hb = ts // CONV_HALO
    last = seq // CONV_HALO - 1
    vec = pl.BlockSpec((1, c), lambda b, i: (0, 0))
    return pl.pallas_call(
        _conv_kernel,
        grid=(bsz, seq // ts),
        in_specs=[
            pl.BlockSpec((1, CONV_HALO, c), lambda b, i: (b, jnp.maximum(i * hb - 1, 0), 0)),
            pl.BlockSpec((1, ts, c), lambda b, i: (b, i, 0)),
            pl.BlockSpec((1, CONV_HALO, c), lambda b, i: (b, jnp.minimum((i + 1) * hb, last), 0)),
            pl.BlockSpec((CONV_K, c), lambda b, i: (0, 0)),
            vec, vec, vec,
        ],
        out_specs=pl.BlockSpec((1, ts, c), lambda b, i: (b, i, 0)),
        out_shape=jax.ShapeDtypeStruct((bsz, seq, c), BF16),
        scratch_shapes=[pltpu.VMEM((ts + 2 * CONV_HALO, c), F32)],
        compiler_params=_params("parallel", "parallel"),
        name="conv",
    )(u, u, u, dw, dw_b.reshape(1, c), ln_g.reshape(1, c), ln_b.reshape(1, c))


def _attn_kernel(q_ref, k_ref, v_ref, o_ref):
    tq = q_ref.shape[1]
    nk = k_ref.shape[1] // TK_ATTN
    for g in range(GROUP):
        sl = slice(g * HEAD_DIM, (g + 1) * HEAD_DIM)
        q = q_ref[0, :, sl]

        def body(c, carry):
            m, l, acc = carry
            start = pl.multiple_of(c * TK_ATTN, TK_ATTN)
            k = k_ref[0, pl.ds(start, TK_ATTN), :]
            v = v_ref[0, pl.ds(start, TK_ATTN), :]
            s = lax.dot_general(q, k, (((1,), (1,)), ((), ())), preferred_element_type=F32)
            m_new = jnp.maximum(m, jnp.max(s, axis=-1, keepdims=True))
            p = jnp.exp(s - m_new)
            alpha = jnp.exp(m - m_new)
            l = alpha * l + jnp.sum(p, axis=-1, keepdims=True)
            acc = alpha * acc + _dot(p.astype(BF16), v)
            return m_new, l, acc

        init = (jnp.full((tq, 1), -jnp.inf, F32), jnp.zeros((tq, 1), F32), jnp.zeros((tq, HEAD_DIM), F32))
        _, l, acc = lax.fori_loop(0, nk, body, init)
        o_ref[0, :, sl] = (acc / l).astype(BF16)


def _attention(q, k, v):
    bsz, seq, _ = q.shape
    gw = GROUP * HEAD_DIM
    return pl.pallas_call(
        _attn_kernel,
        grid=(bsz, N_KV_HEADS, seq // TQ_ATTN),
        in_specs=[
            pl.BlockSpec((1, TQ_ATTN, gw), lambda b, h, i: (b, i, h)),
            pl.BlockSpec((1, seq, HEAD_DIM), lambda b, h, i: (b, 0, h)),
            pl.BlockSpec((1, seq, HEAD_DIM), lambda b, h, i: (b, 0, h)),
        ],
        out_specs=pl.BlockSpec((1, TQ_ATTN, gw), lambda b, h, i: (b, i, h)),
        out_shape=jax.ShapeDtypeStruct((bsz, seq, Q_W), BF16),
        compiler_params=_params("parallel", "parallel", "parallel"),
        name="attn",
    )(q, k, v)


def _merge_kernel(alpha, h_ref, x_ref, c_ref, a_ref, wgc_ref, wga_ref, wpw_ref, wo_ref, wout_ref,
                  bgc_ref, bga_ref, bpw_ref, bout_ref, lg_ref, lb_ref, wrh_ref, wrl_ref,
                  x1_ref, x1b_ref, lgt_ref):
    h = h_ref[...]
    gc = _sigmoid(_dot(h, wgc_ref[...]) + bgc_ref[...])
    ga = _sigmoid(_dot(h, wga_ref[...]) + bga_ref[...])
    yc = _dot(c_ref[...], wpw_ref[...]) + bpw_ref[...]
    ya = _dot(a_ref[...], wo_ref[...])
    mix = _dot((gc * yc + ga * ya).astype(BF16), wout_ref[...]) + bout_ref[...]
    x1 = _layer_norm(alpha * x_ref[...] + mix, lg_ref[...], lb_ref[...])
    x1_ref[...] = x1
    hi = x1.astype(BF16)
    lo = (x1 - hi.astype(F32)).astype(BF16)
    x1b_ref[...] = hi
    lgt_ref[...] = _dot(hi, wrh_ref[...]) + _dot(lo, wrh_ref[...]) + _dot(hi, wrl_ref[...])


def _merge(alpha, hb, x, c, a, wgc, wga, wpw, wo, wout, bgc, bga, bpw, bout, lg, lb, wrh, wrl):
    m, d = x.shape
    tm = TM_MERGE
    row = pl.BlockSpec((tm, d), lambda i: (i, 0))
    w = pl.BlockSpec((d, d), lambda i: (0, 0))
    vec = pl.BlockSpec((1, d), lambda i: (0, 0))
    wr = pl.BlockSpec((d, LANES), lambda i: (0, 0))
    vecs = [v.reshape(1, d) for v in (bgc, bga, bpw, bout, lg, lb)]
    return pl.pallas_call(
        functools.partial(_merge_kernel, alpha),
        grid=(m // tm,),
        in_specs=[row] * 4 + [w] * 5 + [vec] * 6 + [wr, wr],
        out_specs=[row, row, pl.BlockSpec((tm, LANES), lambda i: (i, 0))],
        out_shape=[jax.ShapeDtypeStruct((m, d), F32), jax.ShapeDtypeStruct((m, d), BF16),
                   jax.ShapeDtypeStruct((m, LANES), F32)],
        compiler_params=_params("parallel"),
        name="merge",
    )(hb, x, c, a, wgc, wga, wpw, wo, wout, *vecs, wrh, wrl)


def _route_kernel(cap, lg_ref, pos_ref, gate_ref, cum_ref):
    lg = lg_ref[0]
    ne, nb, _ = lg.shape
    rows = ne * nb
    ex = jnp.exp(lg - jnp.max(lg, axis=0, keepdims=True))
    aff = ex / jnp.sum(ex, axis=0, keepdims=True)

    def count(mask):
        x = jnp.where(mask, 1.0, 0.0)
        return jnp.sum(jnp.sum(x, axis=1, keepdims=True), axis=2, keepdims=True)

    def bit_step(i, tb):
        cand = tb | jnp.left_shift(jnp.int32(1), 30 - i)
        ok = count(aff >= lax.bitcast_convert_type(cand, F32)) >= cap
        return jnp.where(ok, cand, tb)

    tb = lax.fori_loop(0, 31, bit_step, jnp.zeros((ne, 1, 1), jnp.int32))
    above = aff >= lax.bitcast_convert_type(tb + 1, F32)
    tied = jnp.logical_and(aff >= lax.bitcast_convert_type(tb, F32), jnp.logical_not(above))

    li = lax.broadcasted_iota(jnp.int32, (LANES, LANES), 0)
    lj = lax.broadcasted_iota(jnp.int32, (LANES, LANES), 1)
    upper = jnp.where(li <= lj, 1.0, 0.0).astype(BF16)
    ones = jnp.ones((LANES, LANES), BF16)
    ri = lax.broadcasted_iota(jnp.int32, (rows, rows), 0)
    rj = lax.broadcasted_iota(jnp.int32, (rows, rows), 1)
    shift = nb.bit_length() - 1
    assert 1 << shift == nb, "token blocks per sequence must be a power of two"
    same = jnp.right_shift(ri, shift) == jnp.right_shift(rj, shift)
    before = jnp.where(jnp.logical_and(same, rj < ri), 1.0, 0.0).astype(BF16)

    def prefix(mask):
        x = jnp.where(mask, 1.0, 0.0).reshape(rows, LANES)
        xb = x.astype(BF16)
        incl = _dot(xb, upper)
        tot = _dot(xb, ones)
        off = _dot(before, tot.astype(BF16))
        return (off + incl - x).reshape(ne, nb, LANES), off.reshape(ne, nb, LANES)

    need = cap - count(above)
    tie_rank, _ = prefix(tied)
    chosen = jnp.logical_or(above, jnp.logical_and(tied, tie_rank < need))
    pos, cum = prefix(chosen)
    pos_ref[0] = jnp.where(chosen, pos, -1.0)
    gate_ref[0] = jnp.where(chosen, aff, 0.0)
    cum_ref[0] = cum


def _route(logits_t, cap):
    bsz, ne, nb, _ = logits_t.shape
    blk = pl.BlockSpec((1, ne, nb, LANES), lambda b: (b, 0, 0, 0))
    shp = jax.ShapeDtypeStruct(logits_t.shape, F32)
    return pl.pallas_call(
        functools.partial(_route_kernel, cap),
        grid=(bsz,),
        in_specs=[blk],
        out_specs=[blk, blk, blk],
        out_shape=[shp, shp, shp],
        compiler_params=_params("parallel"),
        name="route",
    )(logits_t)


def _gather_kernel(klo_ref, khi_ref, x_ref, pos_ref, xe_ref, acc_ref):
    b = pl.program_id(0)
    e = pl.program_id(1)
    ne = pl.num_programs(1)
    nj = xe_ref.shape[2] // LANES
    slot = lax.broadcasted_iota(jnp.int32, (LANES, LANES), 0).astype(F32)
    for j in range(nj):
        acc_ref[...] = jnp.zeros_like(acc_ref)
        t = (b * ne + e) * nj + j

        def body(k, carry):
            p = pos_ref[0, 0, pl.ds(k, 1), :]
            onehot = jnp.where(slot + float(j * LANES) == p, 1.0, 0.0).astype(BF16)
            xk = x_ref[0, pl.ds(pl.multiple_of(k * LANES, LANES), LANES), :]
            acc_ref[...] += _dot(onehot, xk)
            return carry

        lax.fori_loop(klo_ref[t], khi_ref[t], body, 0)
        xe_ref[0, 0, j * LANES:(j + 1) * LANES, :] = acc_ref[...].astype(BF16)


def _gather(klo, khi, x1b, pos, cap):
    bsz, seq, d = x1b.shape
    ne, nb = pos.shape[1], pos.shape[2]
    grid_spec = pltpu.PrefetchScalarGridSpec(
        num_scalar_prefetch=2,
        grid=(bsz, ne),
        in_specs=[
            pl.BlockSpec((1, seq, d), lambda b, e, *_: (b, 0, 0)),
            pl.BlockSpec((1, 1, nb, LANES), lambda b, e, *_: (b, e, 0, 0)),
        ],
        out_specs=pl.BlockSpec((1, 1, cap, d), lambda b, e, *_: (b, e, 0, 0)),
        scratch_shapes=[pltpu.VMEM((LANES, d), F32)],
    )
    return pl.pallas_call(
        _gather_kernel,
        grid_spec=grid_spec,
        out_shape=jax.ShapeDtypeStruct((bsz, ne, cap, d), BF16),
        compiler_params=_params("parallel", "parallel"),
        name="gather",
    )(klo, khi, x1b, pos)


def _ffn_kernel(xe_ref, wg_ref, wu_ref, wd_ref, y_ref, acc_ref):
    f = pl.program_id(1)
    nf = pl.num_programs(1)
    bsz = xe_ref.shape[0]
    wg = wg_ref[0, 0].astype(BF16)
    wu = wu_ref[0, 0].astype(BF16)
    wd = wd_ref[0, 0].astype(BF16)

    @pl.when(f == 0)
    def _():
        acc_ref[...] = jnp.zeros_like(acc_ref)

    for bi in range(bsz):
        x = xe_ref[bi, 0]
        a = _dot(x, wg)
        u = _dot(x, wu)
        hid = (a * _sigmoid(a) * u).astype(BF16)
        acc_ref[bi] += _dot(hid, wd)

    @pl.when(f == nf - 1)
    def _():
        y_ref[:, 0] = acc_ref[...].astype(BF16)


def _ffn(layer, xe, w_gate, w_up, w_down):
    bsz, ne, cap, d = xe.shape
    ff = w_gate.shape[3]
    tf = TF_FFN
    return pl.pallas_call(
        _ffn_kernel,
        grid=(ne, ff // tf),
        in_specs=[
            pl.BlockSpec((bsz, 1, cap, d), lambda e, f: (0, e, 0, 0)),
            pl.BlockSpec((1, 1, d, tf), lambda e, f: (layer, e, 0, f)),
            pl.BlockSpec((1, 1, d, tf), lambda e, f: (layer, e, 0, f)),
            pl.BlockSpec((1, 1, tf, d), lambda e, f: (layer, e, f, 0)),
        ],
        out_specs=pl.BlockSpec((bsz, 1, cap, d), lambda e, f: (0, e, 0, 0)),
        out_shape=jax.ShapeDtypeStruct((bsz, ne, cap, d), BF16),
        scratch_shapes=[pltpu.VMEM((bsz, cap, d), F32)],
        compiler_params=_params("parallel", "arbitrary"),
        name="ffn",
    )(xe, w_gate, w_up, w_down)


def _combine_kernel(alpha, jlo_ref, jhi_ref, y_ref, post_ref, gatet_ref, x1_ref, g_ref, b_ref,
                    xo_ref, xbo_ref, moe_ref):
    b = pl.program_id(0)
    i = pl.program_id(1)
    ni = pl.num_programs(1)
    ne = y_ref.shape[1]
    tt = x1_ref.shape[1]
    lane = lax.broadcasted_iota(jnp.int32, (tt, LANES), 1).astype(F32)
    moe_ref[...] = jnp.zeros_like(moe_ref)
    for e in range(ne):
        pcol = post_ref[0, :, e:e + 1]
        gcol = gatet_ref[0, :, e:e + 1]
        t = (b * ne + e) * ni + i

        def body(j, carry):
            onehot = jnp.where(pcol - (j * LANES).astype(F32) == lane, 1.0, 0.0).astype(BF16)
            yk = y_ref[0, e, pl.ds(pl.multiple_of(j * LANES, LANES), LANES), :]
            moe_ref[...] += gcol * _dot(onehot, yk)
            return carry

        lax.fori_loop(jlo_ref[t], jhi_ref[t], body, 0)
    y = _layer_norm(alpha * x1_ref[0] + moe_ref[...], g_ref[...], b_ref[...])
    xo_ref[0] = y
    xbo_ref[0] = y.astype(BF16)


def _combine(alpha, jlo, jhi, y, pos_t, gate_t, x1, g, b):
    bsz, seq, d = x1.shape
    ne, cap = y.shape[1], y.shape[2]
    tt = TT_COMB
    tok = pl.BlockSpec((1, tt, d), lambda bb, i, *_: (bb, i, 0))
    sel = pl.BlockSpec((1, tt, ne), lambda bb, i, *_: (bb, i, 0))
    vec = pl.BlockSpec((1, d), lambda bb, i, *_: (0, 0))
    grid_spec = pltpu.PrefetchScalarGridSpec(
        num_scalar_prefetch=2,
        grid=(bsz, seq // tt),
        in_specs=[pl.BlockSpec((1, ne, cap, d), lambda bb, i, *_: (bb, 0, 0, 0)), sel, sel, tok, vec, vec],
        out_specs=[tok, tok],
        scratch_shapes=[pltpu.VMEM((tt, d), F32)],
    )
    return pl.pallas_call(
        functools.partial(_combine_kernel, alpha),
        grid_spec=grid_spec,
        out_shape=[jax.ShapeDtypeStruct((bsz, seq, d), F32), jax.ShapeDtypeStruct((bsz, seq, d), BF16)],
        compiler_params=_params("parallel", "parallel"),
        name="combine",
    )(jlo, jhi, y, pos_t, gate_t, x1, g.reshape(1, d), b.reshape(1, d))


def _rope_tables(seq):
    rows = seq // GRID_W
    row = jnp.repeat(jnp.arange(rows, dtype=jnp.int32), GRID_W).astype(F32)
    col = jnp.tile(jnp.arange(GRID_W, dtype=jnp.int32), rows).astype(F32)
    axis_dim = HEAD_DIM // 2
    freqs = 1.0 / (ROPE_THETA ** (jnp.arange(0, axis_dim, 2, dtype=F32) / axis_dim))
    ang = jnp.concatenate([row[:, None] * freqs[None], col[:, None] * freqs[None]], axis=-1)
    cos, sin = jnp.cos(ang), jnp.sin(ang)
    return jnp.concatenate([cos, cos], axis=-1), jnp.concatenate([-sin, sin], axis=-1)


def _deinterleave_heads(w, n_heads):
    lead = w.shape[:-1]
    w = w.reshape(lead + (n_heads, HEAD_DIM // 2, 2))
    return jnp.swapaxes(w, -1, -2).reshape(lead + (n_heads * HEAD_DIM,))


def _loop_tables(cum, cap, seq):
    bsz, ne, nb = cum.shape
    cum_ext = jnp.concatenate([cum, jnp.full((bsz, ne, 1), cap, jnp.int32)], axis=-1)
    nj = cap // LANES
    slot0 = (jnp.arange(nj, dtype=jnp.int32) * LANES)[None, None, :, None]
    klo = jnp.sum((cum_ext[:, :, None, 1:] <= slot0).astype(jnp.int32), axis=-1)
    khi = jnp.sum((cum_ext[:, :, None, :nb] < slot0 + LANES).astype(jnp.int32), axis=-1)
    per = TT_COMB // LANES
    s0 = cum_ext[:, :, 0:nb:per]
    s1 = cum_ext[:, :, per::per]
    jlo = s0 // LANES
    jhi = jnp.where(s1 > s0, (s1 - 1) // LANES + 1, jlo)
    return klo.reshape(-1), khi.reshape(-1), jlo.reshape(-1), jhi.reshape(-1)


def kernel(x, ln0_g, ln0_b, w_in, b_in, conv_dw, conv_dw_b, conv_ln_g, conv_ln_b, conv_pw_w, conv_pw_b, q_norm_g, k_norm_g, w_o, w_out, b_out, ln1_g, ln1_b, w_router, w_gate, w_up, w_down, ln2_g, ln2_b):
    bsz, seq, d = x.shape
    depth = w_in.shape[0]
    alpha = (2.0 * depth) ** 0.25
    cap = EC_FACTOR * seq // N_EXPERTS
    m = bsz * seq
    nb = seq // LANES

    off_q = 2 * C_CONV
    off_k = off_q + Q_W
    off_v = off_k + KV_W
    off_gc = off_v + KV_W
    off_ga = off_gc + d

    cos, sin = _rope_tables(seq)
    scale = 1.0 / math.sqrt(HEAD_DIM)
    cq, sq = cos * scale, sin * scale
    gq = _deinterleave_heads(q_norm_g, 1)
    gk = _deinterleave_heads(k_norm_g, 1)

    w_in_b = w_in.astype(BF16)
    w_pw_b = conv_pw_w.astype(BF16)
    w_o_b = w_o.astype(BF16)
    w_out_b = w_out.astype(BF16)
    w_r = jnp.pad(w_router, ((0, 0), (0, 0), (0, LANES - N_EXPERTS)))
    w_r_hi = w_r.astype(BF16)
    w_r_lo = (w_r - w_r_hi.astype(F32)).astype(BF16)

    xf, xb = _ln0(x.reshape(m, d), ln0_g, ln0_b)
    for l in range(depth):
        wl, bl = w_in_b[l], b_in[l]
        u = _glu(xb, wl[:, :C_CONV], wl[:, C_CONV:off_q], bl[:C_CONV], bl[C_CONV:off_q])
        q, k, v = _qkv(
            xb, _deinterleave_heads(wl[:, off_q:off_k], N_HEADS), _deinterleave_heads(wl[:, off_k:off_v], N_KV_HEADS),
            wl[:, off_v:off_gc], _deinterleave_heads(bl[off_q:off_k], N_HEADS),
            _deinterleave_heads(bl[off_k:off_v], N_KV_HEADS), bl[off_v:off_gc],
            gq[l], gk[l], cq, sq, cos, sin, seq)
        c = _conv(u.reshape(bsz, seq, C_CONV), conv_dw[l], conv_dw_b[l], conv_ln_g[l], conv_ln_b[l])
        a = _attention(q.reshape(bsz, seq, Q_W), k.reshape(bsz, seq, KV_W), v.reshape(bsz, seq, KV_W))
        x1, x1b, logits = _merge(
            alpha, xb, xf, c.reshape(m, d), a.reshape(m, d),
            wl[:, off_gc:off_ga], wl[:, off_ga:], w_pw_b[l], w_o_b[l], w_out_b[l],
            bl[off_gc:off_ga], bl[off_ga:], conv_pw_b[l], b_out[l], ln1_g[l], ln1_b[l], w_r_hi[l], w_r_lo[l])
        logits_t = jnp.swapaxes(logits[:, :N_EXPERTS].reshape(bsz, seq, N_EXPERTS), 1, 2)
        pos, gate, cum = _route(logits_t.reshape(bsz, N_EXPERTS, nb, LANES), cap)
        klo, khi, jlo, jhi = _loop_tables(cum[..., 0].astype(jnp.int32), cap, seq)
        xe = _gather(klo, khi, x1b.reshape(bsz, seq, d), pos, cap)
        y = _ffn(l, xe, w_gate, w_up, w_down)
        pos_t = jnp.swapaxes(pos.reshape(bsz, N_EXPERTS, seq), 1, 2)
        gate_t = jnp.swapaxes(gate.reshape(bsz, N_EXPERTS, seq), 1, 2)
        xo, xbo = _combine(alpha, jlo, jhi, y, pos_t, gate_t, x1.reshape(bsz, seq, d), ln2_g[l], ln2_b[l])
        xf, xb = xo.reshape(m, d), xbo.reshape(m, d)
    return xf.reshape(bsz, seq, d)
```

```python
import functools
import math

import jax
import jax.numpy as jnp
from jax import lax
from jax.experimental import pallas as pl
from jax.experimental.pallas import tpu as pltpu

F32 = jnp.float32
BF16 = jnp.bfloat16

D_MODEL = 1024
N_HEADS = 8
N_KV_HEADS = 2
HEAD_DIM = 128
GROUP = N_HEADS // N_KV_HEADS
GRID_W = 64
ROPE_THETA = 10000.0
C_CONV = D_MODEL
CONV_K = 31
CONV_HALO = 16
N_EXPERTS = 16
EC_FACTOR = 2
LN_EPS = 1e-5
RMS_EPS = 1e-6
Q_W = N_HEADS * HEAD_DIM
KV_W = N_KV_HEADS * HEAD_DIM
AUG_DIM = 2 * HEAD_DIM
KV_AUG_W = N_KV_HEADS * AUG_DIM
MAX_FIXED_SHIFT = 60.0

LANES = 128
SUBLANES = 8
VMEM_LIMIT = 56 * 1024 * 1024

TM_LN = 512
TM_PROJ = 512
TS_CONV = 256
RC_CONV = 32
TQ_ATTN = 256
TK_ATTN = 512
TM_MERGE = 512
TF_FFN = 512
TT_COMB = 256
WIN_COMB = 256
TC_GATHER = 512


def _params(*sem):
    return pltpu.CompilerParams(dimension_semantics=sem, vmem_limit_bytes=VMEM_LIMIT)


def _sigmoid(x):
    return 1.0 / (1.0 + jnp.exp(-x))


def _layer_norm(x, g, b):
    mu = jnp.mean(x, axis=-1, keepdims=True)
    xc = x - mu
    var = jnp.mean(xc * xc, axis=-1, keepdims=True)
    return xc * lax.rsqrt(var + LN_EPS) * g + b


def _dot(a, b):
    return jnp.dot(a, b, preferred_element_type=F32)


def _ln0_kernel(x_ref, g_ref, b_ref, xf_ref, xb_ref):
    y = _layer_norm(x_ref[...], g_ref[...], b_ref[...])
    xf_ref[...] = y
    xb_ref[...] = y.astype(BF16)


def _ln0(x2d, g, b):
    m, d = x2d.shape
    row = pl.BlockSpec((TM_LN, d), lambda i: (i, 0))
    vec = pl.BlockSpec((1, d), lambda i: (0, 0))
    return pl.pallas_call(
        _ln0_kernel,
        grid=(m // TM_LN,),
        in_specs=[row, vec, vec],
        out_specs=[row, row],
        out_shape=[jax.ShapeDtypeStruct((m, d), F32), jax.ShapeDtypeStruct((m, d), BF16)],
        compiler_params=_params("parallel"),
        name="ln0",
    )(x2d, g.reshape(1, d), b.reshape(1, d))


def _glu_kernel(h_ref, wv_ref, wg_ref, bv_ref, bg_ref, u_ref):
    h = h_ref[...]
    val = _dot(h, wv_ref[...]) + bv_ref[...]
    gate = _dot(h, wg_ref[...]) + bg_ref[...]
    u_ref[...] = val * _sigmoid(gate)


def _glu(hb, wv, wg, bv, bg):
    m, d = hb.shape
    c = wv.shape[1]
    row_in = pl.BlockSpec((TM_PROJ, d), lambda i: (i, 0))
    w = pl.BlockSpec((d, c), lambda i: (0, 0))
    vec = pl.BlockSpec((1, c), lambda i: (0, 0))
    return pl.pallas_call(
        _glu_kernel,
        grid=(m // TM_PROJ,),
        in_specs=[row_in, w, w, vec, vec],
        out_specs=pl.BlockSpec((TM_PROJ, c), lambda i: (i, 0)),
        out_shape=jax.ShapeDtypeStruct((m, c), F32),
        compiler_params=_params("parallel"),
        name="glu",
    )(hb, wv, wg, bv.reshape(1, c), bg.reshape(1, c))


def _norm_rope(x, g, cos, sin):
    ms = jnp.mean(x * x, axis=-1, keepdims=True)
    xn = x * lax.rsqrt(ms + RMS_EPS) * g
    return xn * cos + pltpu.roll(xn, HEAD_DIM // 2, 1) * sin


def _qkv_kernel(h_ref, wq_ref, wk_ref, wv_ref, bq_ref, bk_ref, bv_ref, gq_ref, gk_ref,
                cq_ref, sq_ref, ck_ref, sk_ref, q_ref, k_ref, v_ref):
    h = h_ref[...]
    zq = _dot(h, wq_ref[...]) + bq_ref[...]
    zk = _dot(h, wk_ref[...]) + bk_ref[...]
    zv = _dot(h, wv_ref[...]) + bv_ref[...]
    cq, sq, ck, sk = cq_ref[...], sq_ref[...], ck_ref[...], sk_ref[...]
    gq, gk = gq_ref[...], gk_ref[...]
    for hd in range(N_HEADS):
        sl = slice(hd * HEAD_DIM, (hd + 1) * HEAD_DIM)
        q_ref[:, sl] = _norm_rope(zq[:, sl], gq, cq, sq).astype(BF16)
    one_col = jnp.where(lax.broadcasted_iota(jnp.int32, (h.shape[0], AUG_DIM - HEAD_DIM), 1) == 0, 1.0, 0.0).astype(BF16)
    for hd in range(N_KV_HEADS):
        sl = slice(hd * HEAD_DIM, (hd + 1) * HEAD_DIM)
        k_ref[:, hd * AUG_DIM:hd * AUG_DIM + HEAD_DIM] = _norm_rope(zk[:, sl], gk, ck, sk).astype(BF16)
        k_ref[:, hd * AUG_DIM + HEAD_DIM:(hd + 1) * AUG_DIM] = one_col
        v_ref[:, hd * AUG_DIM:hd * AUG_DIM + HEAD_DIM] = zv[:, sl].astype(BF16)
        v_ref[:, hd * AUG_DIM + HEAD_DIM:(hd + 1) * AUG_DIM] = one_col


def _qkv(hb, wq, wk, wv, bq, bk, bv, gq, gk, cq, sq, ck, sk, seq):
    m, d = hb.shape
    tm = TM_PROJ
    per_seq = seq // tm
    row = lambda w: pl.BlockSpec((tm, w), lambda i: (i, 0))
    full = lambda a: pl.BlockSpec(a.shape, lambda i: (0, 0))
    tab = pl.BlockSpec((tm, HEAD_DIM), lambda i: (i % per_seq, 0))
    args = (hb, wq, wk, wv, bq.reshape(1, -1), bk.reshape(1, -1), bv.reshape(1, -1),
            gq.reshape(1, -1), gk.reshape(1, -1), cq, sq, ck, sk)
    in_specs = [row(d)] + [full(a) for a in args[1:9]] + [tab] * 4
    return pl.pallas_call(
        _qkv_kernel,
        grid=(m // tm,),
        in_specs=in_specs,
        out_specs=[row(Q_W), row(KV_AUG_W), row(KV_AUG_W)],
        out_shape=[jax.ShapeDtypeStruct((m, Q_W), BF16), jax.ShapeDtypeStruct((m, KV_AUG_W), BF16),
                   jax.ShapeDtypeStruct((m, KV_AUG_W), BF16)],
        compiler_params=_params("parallel"),
        name="qkv",
    )(*args)


def _conv_kernel(up_ref, uc_ref, un_ref, w_ref, wb_ref, g_ref, b_ref, o_ref, ext_ref, sh_ref):
    i = pl.program_id(1)
    n = pl.num_programs(1)
    ts = uc_ref.shape[1]
    ext_ref[0:CONV_HALO, :] = jnp.where(i > 0, up_ref[0], 0.0)
    ext_ref[CONV_HALO:CONV_HALO + ts, :] = uc_ref[0]
    ext_ref[CONV_HALO + ts:, :] = jnp.where(i < n - 1, un_ref[0], 0.0)
    span = sh_ref.shape[1]
    for s in range(SUBLANES):
        sh_ref[s] = ext_ref[s:s + span, :]
    g, b = g_ref[...], b_ref[...]
    first = CONV_HALO - CONV_K // 2
    for r0 in range(0, ts, RC_CONV):
        acc = jnp.zeros((RC_CONV, uc_ref.shape[2]), F32) + wb_ref[...]
        for k in range(CONV_K):
            a, s = divmod(first + k, SUBLANES)
            acc = acc + w_ref[k:k + 1, :] * sh_ref[s, r0 + a * SUBLANES:r0 + a * SUBLANES + RC_CONV, :]
        y = _layer_norm(acc, g, b)
        o_ref[0, r0:r0 + RC_CONV, :] = (y * _sigmoid(y)).astype(BF16)


def _conv(u, dw, dw_b, ln_g, ln_b):
    bsz, seq, c = u.shape
    ts = TS_CONV
    hb = ts // CONV_HALO
    last = seq // CONV_HALO - 1
    vec = pl.BlockSpec((1, c), lambda b, i: (0, 0))
    return pl.pallas_call(
        _conv_kernel,
        grid=(bsz, seq // ts),
        in_specs=[
            pl.BlockSpec((1, CONV_HALO, c), lambda b, i: (b, jnp.maximum(i * hb - 1, 0), 0)),
            pl.BlockSpec((1, ts, c), lambda b, i: (b, i, 0)),
            pl.BlockSpec((1, CONV_HALO, c), lambda b, i: (b, jnp.minimum((i + 1) * hb, last), 0)),
            pl.BlockSpec((CONV_K, c), lambda b, i: (0, 0)),
            vec, vec, vec,
        ],
        out_specs=pl.BlockSpec((1, ts, c), lambda b, i: (b, i, 0)),
        out_shape=jax.ShapeDtypeStruct((bsz, seq, c), BF16),
        scratch_shapes=[pltpu.VMEM((ts + 2 * CONV_HALO, c), F32),
                        pltpu.VMEM((SUBLANES, ts + 2 * CONV_HALO - SUBLANES, c), F32)],
        compiler_params=_params("parallel", "parallel"),
        name="conv",
    )(u, u, u, dw, dw_b.reshape(1, c), ln_g.reshape(1, c), ln_b.reshape(1, c))


def _attn_kernel(q_ref, k_ref, v_ref, o_ref, kmax_ref):
    i = pl.program_id(2)
    tq = q_ref.shape[1]
    nk = k_ref.shape[1] // TK_ATTN
    rows = GROUP * tq
    pad = AUG_DIM - HEAD_DIM

    @pl.when(i == 0)
    def _():
        kk = k_ref[0].astype(F32)
        ksq = jnp.sum(kk * kk, axis=1, keepdims=True) - 1.0
        kmax_ref[...] = jnp.max(ksq, axis=0, keepdims=True)

    q = jnp.concatenate([q_ref[0, :, g * HEAD_DIM:(g + 1) * HEAD_DIM] for g in range(GROUP)], axis=0)
    qf = q.astype(F32)
    shift = jnp.sqrt(jnp.sum(qf * qf, axis=1, keepdims=True) * kmax_ref[...])
    first = lax.broadcasted_iota(jnp.int32, (rows, pad), 1) == 0
    nt = (((1,), (1,)), ((), ()))

    def chunk(c):
        start = pl.multiple_of(c * TK_ATTN, TK_ATTN)
        return k_ref[0, pl.ds(start, TK_ATTN), :], v_ref[0, pl.ds(start, TK_ATTN), :]

    def finish(acc):
        out = (acc[:, :HEAD_DIM] / acc[:, HEAD_DIM:HEAD_DIM + 1]).astype(BF16)
        for g in range(GROUP):
            o_ref[0, :, g * HEAD_DIM:(g + 1) * HEAD_DIM] = out[g * tq:(g + 1) * tq]

    def fixed_shift():
        qa = jnp.concatenate([q, jnp.where(first, -shift, 0.0).astype(BF16)], axis=1)

        def body(c, acc):
            k, v = chunk(c)
            s = lax.dot_general(qa, k, nt, preferred_element_type=F32)
            return acc + _dot(jnp.exp2(s).astype(BF16), v)

        finish(lax.fori_loop(0, nk, body, jnp.zeros((rows, AUG_DIM), F32), unroll=True))

    def running_max():
        qa = jnp.concatenate([q, jnp.zeros((rows, pad), BF16)], axis=1)

        def body(c, carry):
            m, acc = carry
            k, v = chunk(c)
            s = lax.dot_general(qa, k, nt, preferred_element_type=F32)
            m_new = jnp.maximum(m, jnp.max(s, axis=-1, keepdims=True))
            acc = jnp.exp2(m - m_new) * acc + _dot(jnp.exp2(s - m_new).astype(BF16), v)
            return m_new, acc

        init = (jnp.full((rows, 1), -jnp.inf, F32), jnp.zeros((rows, AUG_DIM), F32))
        finish(lax.fori_loop(0, nk, body, init)[1])

    lax.cond(jnp.max(shift) <= MAX_FIXED_SHIFT, fixed_shift, running_max)


def _attention(q, k, v):
    bsz, seq, _ = q.shape
    gw = GROUP * HEAD_DIM
    kv = pl.BlockSpec((1, seq, AUG_DIM), lambda b, h, i: (b, 0, h))
    return pl.pallas_call(
        _attn_kernel,
        grid=(bsz, N_KV_HEADS, seq // TQ_ATTN),
        in_specs=[pl.BlockSpec((1, TQ_ATTN, gw), lambda b, h, i: (b, i, h)), kv, kv],
        out_specs=pl.BlockSpec((1, TQ_ATTN, gw), lambda b, h, i: (b, i, h)),
        out_shape=jax.ShapeDtypeStruct((bsz, seq, Q_W), BF16),
        scratch_shapes=[pltpu.VMEM((1, 1), F32)],
        compiler_params=_params("parallel", "parallel", "arbitrary"),
        name="attn",
    )(q, k, v)


def _merge_kernel(alpha, h_ref, x_ref, c_ref, a_ref, wgc_ref, wga_ref, wpw_ref, wo_ref, wout_ref,
                  bgc_ref, bga_ref, bpw_ref, bout_ref, lg_ref, lb_ref, wrh_ref, wrl_ref,
                  x1_ref, x1b_ref, lgt_ref):
    h = h_ref[...]
    gc = _sigmoid(_dot(h, wgc_ref[...]) + bgc_ref[...])
    ga = _sigmoid(_dot(h, wga_ref[...]) + bga_ref[...])
    yc = _dot(c_ref[...], wpw_ref[...]) + bpw_ref[...]
    ya = _dot(a_ref[...], wo_ref[...])
    mix = _dot((gc * yc + ga * ya).astype(BF16), wout_ref[...]) + bout_ref[...]
    x1 = _layer_norm(alpha * x_ref[...] + mix, lg_ref[...], lb_ref[...])
    x1_ref[...] = x1
    hi = x1.astype(BF16)
    lo = (x1 - hi.astype(F32)).astype(BF16)
    x1b_ref[...] = hi
    lgt_ref[...] = _dot(hi, wrh_ref[...]) + _dot(lo, wrh_ref[...]) + _dot(hi, wrl_ref[...])


def _merge(alpha, hb, x, c, a, wgc, wga, wpw, wo, wout, bgc, bga, bpw, bout, lg, lb, wrh, wrl):
    m, d = x.shape
    tm = TM_MERGE
    row = pl.BlockSpec((tm, d), lambda i: (i, 0))
    w = pl.BlockSpec((d, d), lambda i: (0, 0), pipeline_mode=pl.Buffered(1))
    vec = pl.BlockSpec((1, d), lambda i: (0, 0))
    wr = pl.BlockSpec((d, LANES), lambda i: (0, 0))
    vecs = [v.reshape(1, d) for v in (bgc, bga, bpw, bout, lg, lb)]
    return pl.pallas_call(
        functools.partial(_merge_kernel, alpha),
        grid=(m // tm,),
        in_specs=[row] * 4 + [w] * 5 + [vec] * 6 + [wr, wr],
        out_specs=[row, row, pl.BlockSpec((tm, LANES), lambda i: (i, 0))],
        out_shape=[jax.ShapeDtypeStruct((m, d), F32), jax.ShapeDtypeStruct((m, d), BF16),
                   jax.ShapeDtypeStruct((m, LANES), F32)],
        compiler_params=_params("parallel"),
        name="merge",
    )(hb, x, c, a, wgc, wga, wpw, wo, wout, *vecs, wrh, wrl)


def _route_kernel(cap, lg_ref, pos_ref, gate_ref, cum_ref):
    lg = lg_ref[0]
    ne, nb, _ = lg.shape
    rows = ne * nb
    ex = jnp.exp(lg - jnp.max(lg, axis=0, keepdims=True))
    aff = ex / jnp.sum(ex, axis=0, keepdims=True)

    def count(mask):
        x = jnp.where(mask, 1.0, 0.0)
        return jnp.sum(jnp.sum(x, axis=1, keepdims=True), axis=2, keepdims=True)

    def bit_step(i, tb):
        cand = tb | jnp.left_shift(jnp.int32(1), 30 - i)
        ok = count(aff >= lax.bitcast_convert_type(cand, F32)) >= cap
        return jnp.where(ok, cand, tb)

    tb = lax.fori_loop(0, 31, bit_step, jnp.zeros((ne, 1, 1), jnp.int32))
    above = aff >= lax.bitcast_convert_type(tb + 1, F32)
    tied = jnp.logical_and(aff >= lax.bitcast_convert_type(tb, F32), jnp.logical_not(above))

    li = lax.broadcasted_iota(jnp.int32, (LANES, LANES), 0)
    lj = lax.broadcasted_iota(jnp.int32, (LANES, LANES), 1)
    upper = jnp.where(li <= lj, 1.0, 0.0).astype(BF16)
    ones = jnp.ones((LANES, LANES), BF16)
    ri = lax.broadcasted_iota(jnp.int32, (rows, rows), 0)
    rj = lax.broadcasted_iota(jnp.int32, (rows, rows), 1)
    shift = nb.bit_length() - 1
    assert 1 << shift == nb, "token blocks per sequence must be a power of two"
    same = jnp.right_shift(ri, shift) == jnp.right_shift(rj, shift)
    before = jnp.where(jnp.logical_and(same, rj < ri), 1.0, 0.0).astype(BF16)

    def prefix(mask):
        x = jnp.where(mask, 1.0, 0.0).reshape(rows, LANES)
        xb = x.astype(BF16)
        incl = _dot(xb, upper)
        tot = _dot(xb, ones)
        off = _dot(before, tot.astype(BF16))
        return (off + incl - x).reshape(ne, nb, LANES), off.reshape(ne, nb, LANES)

    need = cap - count(above)
    tie_rank, _ = prefix(tied)
    chosen = jnp.logical_or(above, jnp.logical_and(tied, tie_rank < need))
    pos, cum = prefix(chosen)
    pos_ref[0] = jnp.where(chosen, pos, -1.0)
    gate_ref[0] = jnp.where(chosen, aff, 0.0)
    cum_ref[0] = cum


def _route(logits_t, cap):
    bsz, ne, nb, _ = logits_t.shape
    blk = pl.BlockSpec((1, ne, nb, LANES), lambda b: (b, 0, 0, 0))
    shp = jax.ShapeDtypeStruct(logits_t.shape, F32)
    return pl.pallas_call(
        functools.partial(_route_kernel, cap),
        grid=(bsz,),
        in_specs=[blk],
        out_specs=[blk, blk, blk],
        out_shape=[shp, shp, shp],
        compiler_params=_params("parallel"),
        name="route",
    )(logits_t)


def _gather_kernel(clo_ref, chi_ref, x_ref, pos_ref, gate_ref, xe_ref, gs_ref, acc_ref, gacc_ref):
    b = pl.program_id(0)
    e = pl.program_id(1)
    ne = pl.num_programs(1)
    nj = xe_ref.shape[2] // LANES
    per = TC_GATHER // LANES
    slot = lax.broadcasted_iota(jnp.int32, (LANES, TC_GATHER), 0).astype(F32)
    for j in range(nj):
        acc_ref[...] = jnp.zeros_like(acc_ref)
        gacc_ref[...] = jnp.zeros_like(gacc_ref)
        t = (b * ne + e) * nj + j

        def body(c, carry):
            p = jnp.concatenate([pos_ref[0, 0, pl.ds(c * per + r, 1), :] for r in range(per)], axis=1)
            gt = jnp.concatenate([gate_ref[0, 0, pl.ds(c * per + r, 1), :] for r in range(per)], axis=1)
            match = slot + float(j * LANES) == p
            xk = x_ref[0, pl.ds(pl.multiple_of(c * TC_GATHER, TC_GATHER), TC_GATHER), :]
            acc_ref[...] += _dot(jnp.where(match, 1.0, 0.0).astype(BF16), xk)
            gacc_ref[...] += jnp.sum(jnp.where(match, gt, 0.0), axis=1, keepdims=True)
            return carry

        lax.fori_loop(clo_ref[t], chi_ref[t], body, 0)
        xe_ref[0, 0, j * LANES:(j + 1) * LANES, :] = acc_ref[...].astype(BF16)
        gs_ref[0, 0, j * LANES:(j + 1) * LANES, :] = gacc_ref[...]


def _gather(clo, chi, x1b, pos, gate, cap):
    bsz, seq, d = x1b.shape
    ne, nb = pos.shape[1], pos.shape[2]
    sel = pl.BlockSpec((1, 1, nb, LANES), lambda b, e, *_: (b, e, 0, 0))
    grid_spec = pltpu.PrefetchScalarGridSpec(
        num_scalar_prefetch=2,
        grid=(bsz, ne),
        in_specs=[pl.BlockSpec((1, seq, d), lambda b, e, *_: (b, 0, 0)), sel, sel],
        out_specs=[pl.BlockSpec((1, 1, cap, d), lambda b, e, *_: (b, e, 0, 0)),
                   pl.BlockSpec((1, 1, cap, 1), lambda b, e, *_: (b, e, 0, 0))],
        scratch_shapes=[pltpu.VMEM((LANES, d), F32), pltpu.VMEM((LANES, 1), F32)],
    )
    return pl.pallas_call(
        _gather_kernel,
        grid_spec=grid_spec,
        out_shape=[jax.ShapeDtypeStruct((bsz, ne, cap, d), BF16), jax.ShapeDtypeStruct((bsz, ne, cap, 1), F32)],
        compiler_params=_params("parallel", "parallel"),
        name="gather",
    )(clo, chi, x1b, pos, gate)


def _ffn_kernel(xe_ref, gs_ref, wg_ref, wu_ref, wd_ref, y_ref, acc_ref):
    f = pl.program_id(1)
    nf = pl.num_programs(1)
    bsz = xe_ref.shape[0]
    wg = wg_ref[0, 0].astype(BF16)
    wu = wu_ref[0, 0].astype(BF16)
    wd = wd_ref[0, 0].astype(BF16)

    @pl.when(f == 0)
    def _():
        acc_ref[...] = jnp.zeros_like(acc_ref)

    for bi in range(bsz):
        x = xe_ref[bi, 0]
        a = _dot(x, wg)
        u = _dot(x, wu)
        hid = (a * _sigmoid(a) * u).astype(BF16)
        acc_ref[bi] += _dot(hid, wd)

    @pl.when(f == nf - 1)
    def _():
        y_ref[:, 0] = (acc_ref[...] * gs_ref[:, 0]).astype(BF16)


def _ffn(layer, xe, gs, w_gate, w_up, w_down):
    bsz, ne, cap, d = xe.shape
    ff = w_gate.shape[3]
    tf = TF_FFN
    return pl.pallas_call(
        _ffn_kernel,
        grid=(ne, ff // tf),
        in_specs=[
            pl.BlockSpec((bsz, 1, cap, d), lambda e, f: (0, e, 0, 0)),
            pl.BlockSpec((bsz, 1, cap, 1), lambda e, f: (0, e, 0, 0)),
            pl.BlockSpec((1, 1, d, tf), lambda e, f: (layer, e, 0, f)),
            pl.BlockSpec((1, 1, d, tf), lambda e, f: (layer, e, 0, f)),
            pl.BlockSpec((1, 1, tf, d), lambda e, f: (layer, e, f, 0)),
        ],
        out_specs=pl.BlockSpec((bsz, 1, cap, d), lambda e, f: (0, e, 0, 0)),
        out_shape=jax.ShapeDtypeStruct((bsz, ne, cap, d), BF16),
        scratch_shapes=[pltpu.VMEM((bsz, cap, d), F32)],
        compiler_params=_params("parallel", "arbitrary"),
        name="ffn",
    )(xe, gs, w_gate, w_up, w_down)


def _combine_kernel(alpha, win, st_ref, xlo_ref, xhi_ref, y_ref, post_ref, x1_ref, g_ref, b_ref,
                    xo_ref, xbo_ref, moe_ref):
    b = pl.program_id(0)
    i = pl.program_id(1)
    ni = pl.num_programs(1)
    ne = y_ref.shape[1]
    tt = x1_ref.shape[1]
    lane_w = lax.broadcasted_iota(jnp.int32, (tt, win), 1).astype(F32)
    lane = lax.broadcasted_iota(jnp.int32, (tt, LANES), 1).astype(F32)
    moe = jnp.zeros((tt, x1_ref.shape[2]), F32)
    for e in range(ne):
        st = pl.multiple_of(st_ref[(b * ne + e) * ni + i], LANES)
        rel = post_ref[0, :, e:e + 1] - st.astype(F32)
        onehot = jnp.where(rel == lane_w, 1.0, 0.0).astype(BF16)
        moe = moe + _dot(onehot, y_ref[0, e, pl.ds(st, win), :])
    moe_ref[...] = moe
    for e in range(ne):
        pcol = post_ref[0, :, e:e + 1]
        t = (b * ne + e) * ni + i

        def body(j, carry):
            onehot = jnp.where(pcol - (j * LANES).astype(F32) == lane, 1.0, 0.0).astype(BF16)
            yk = y_ref[0, e, pl.ds(pl.multiple_of(j * LANES, LANES), LANES), :]
            moe_ref[...] += _dot(onehot, yk)
            return carry

        lax.fori_loop(xlo_ref[t], xhi_ref[t], body, 0)
    y = _layer_norm(alpha * x1_ref[0] + moe_ref[...], g_ref[...], b_ref[...])
    xo_ref[0] = y
    xbo_ref[0] = y.astype(BF16)


def _combine(alpha, win, st, xlo, xhi, y, pos_t, x1, g, b):
    bsz, seq, d = x1.shape
    ne, cap = y.shape[1], y.shape[2]
    tt = TT_COMB
    tok = pl.BlockSpec((1, tt, d), lambda bb, i, *_: (bb, i, 0))
    sel = pl.BlockSpec((1, tt, ne), lambda bb, i, *_: (bb, i, 0))
    vec = pl.BlockSpec((1, d), lambda bb, i, *_: (0, 0))
    y_spec = pl.BlockSpec((1, ne, cap, d), lambda bb, i, *_: (bb, 0, 0, 0), pipeline_mode=pl.Buffered(1))
    grid_spec = pltpu.PrefetchScalarGridSpec(
        num_scalar_prefetch=3,
        grid=(bsz, seq // tt),
        in_specs=[y_spec, sel, tok, vec, vec],
        out_specs=[tok, tok],
        scratch_shapes=[pltpu.VMEM((tt, d), F32)],
    )
    return pl.pallas_call(
        functools.partial(_combine_kernel, alpha, win),
        grid_spec=grid_spec,
        out_shape=[jax.ShapeDtypeStruct((bsz, seq, d), F32), jax.ShapeDtypeStruct((bsz, seq, d), BF16)],
        compiler_params=_params("parallel", "arbitrary"),
        name="combine",
    )(st, xlo, xhi, y, pos_t, x1, g.reshape(1, d), b.reshape(1, d))


def _rope_tables(seq):
    rows = seq // GRID_W
    row = jnp.repeat(jnp.arange(rows, dtype=jnp.int32), GRID_W).astype(F32)
    col = jnp.tile(jnp.arange(GRID_W, dtype=jnp.int32), rows).astype(F32)
    axis_dim = HEAD_DIM // 2
    freqs = 1.0 / (ROPE_THETA ** (jnp.arange(0, axis_dim, 2, dtype=F32) / axis_dim))
    ang = jnp.concatenate([row[:, None] * freqs[None], col[:, None] * freqs[None]], axis=-1)
    cos, sin = jnp.cos(ang), jnp.sin(ang)
    return jnp.concatenate([cos, cos], axis=-1), jnp.concatenate([-sin, sin], axis=-1)


def _deinterleave_heads(w, n_heads):
    lead = w.shape[:-1]
    w = w.reshape(lead + (n_heads, HEAD_DIM // 2, 2))
    return jnp.swapaxes(w, -1, -2).reshape(lead + (n_heads * HEAD_DIM,))


def _loop_tables(cum, cap, win):
    bsz, ne, nb = cum.shape
    cum_ext = jnp.concatenate([cum, jnp.full((bsz, ne, 1), cap, jnp.int32)], axis=-1)
    nj = cap // LANES
    slot0 = (jnp.arange(nj, dtype=jnp.int32) * LANES)[None, None, :, None]
    klo = jnp.sum((cum_ext[:, :, None, 1:] <= slot0).astype(jnp.int32), axis=-1)
    khi = jnp.sum((cum_ext[:, :, None, :nb] < slot0 + LANES).astype(jnp.int32), axis=-1)
    per = TC_GATHER // LANES
    clo = klo // per
    chi = (khi + per - 1) // per
    per = TT_COMB // LANES
    s0 = cum_ext[:, :, 0:nb:per]
    s1 = cum_ext[:, :, per::per]
    st = jnp.minimum((s0 // LANES) * LANES, cap - win)
    xlo = (st + win) // LANES
    xhi = jnp.where(s1 > s0, (s1 - 1) // LANES + 1, 0)
    return clo.reshape(-1), chi.reshape(-1), st.reshape(-1), xlo.reshape(-1), xhi.reshape(-1)


def kernel(x, ln0_g, ln0_b, w_in, b_in, conv_dw, conv_dw_b, conv_ln_g, conv_ln_b, conv_pw_w, conv_pw_b, q_norm_g, k_norm_g, w_o, w_out, b_out, ln1_g, ln1_b, w_router, w_gate, w_up, w_down, ln2_g, ln2_b):
    bsz, seq, d = x.shape
    depth = w_in.shape[0]
    alpha = (2.0 * depth) ** 0.25
    cap = EC_FACTOR * seq // N_EXPERTS
    win = min(WIN_COMB, cap)
    m = bsz * seq
    nb = seq // LANES

    off_q = 2 * C_CONV
    off_k = off_q + Q_W
    off_v = off_k + KV_W
    off_gc = off_v + KV_W
    off_ga = off_gc + d

    cos, sin = _rope_tables(seq)
    scale = math.log2(math.e) / math.sqrt(HEAD_DIM)
    cq, sq = cos * scale, sin * scale
    gq = _deinterleave_heads(q_norm_g, 1)
    gk = _deinterleave_heads(k_norm_g, 1)

    w_in_b = w_in.astype(BF16)
    w_pw_b = conv_pw_w.astype(BF16)
    w_o_b = w_o.astype(BF16)
    w_out_b = w_out.astype(BF16)
    w_r = jnp.pad(w_router, ((0, 0), (0, 0), (0, LANES - N_EXPERTS)))
    w_r_hi = w_r.astype(BF16)
    w_r_lo = (w_r - w_r_hi.astype(F32)).astype(BF16)

    xf, xb = _ln0(x.reshape(m, d), ln0_g, ln0_b)
    for l in range(depth):
        wl, bl = w_in_b[l], b_in[l]
        u = _glu(xb, wl[:, :C_CONV], wl[:, C_CONV:off_q], bl[:C_CONV], bl[C_CONV:off_q])
        q, k, v = _qkv(
            xb, _deinterleave_heads(wl[:, off_q:off_k], N_HEADS), _deinterleave_heads(wl[:, off_k:off_v], N_KV_HEADS),
            wl[:, off_v:off_gc], _deinterleave_heads(bl[off_q:off_k], N_HEADS),
            _deinterleave_heads(bl[off_k:off_v], N_KV_HEADS), bl[off_v:off_gc],
            gq[l], gk[l], cq, sq, cos, sin, seq)
        c = _conv(u.reshape(bsz, seq, C_CONV), conv_dw[l], conv_dw_b[l], conv_ln_g[l], conv_ln_b[l])
        a = _attention(q.reshape(bsz, seq, Q_W), k.reshape(bsz, seq, KV_AUG_W), v.reshape(bsz, seq, KV_AUG_W))
        x1, x1b, logits = _merge(
            alpha, xb, xf, c.reshape(m, d), a.reshape(m, d),
            wl[:, off_gc:off_ga], wl[:, off_ga:], w_pw_b[l], w_o_b[l], w_out_b[l],
            bl[off_gc:off_ga], bl[off_ga:], conv_pw_b[l], b_out[l], ln1_g[l], ln1_b[l], w_r_hi[l], w_r_lo[l])
        logits_t = jnp.swapaxes(logits[:, :N_EXPERTS].reshape(bsz, seq, N_EXPERTS), 1, 2)
        pos, gate, cum = _route(logits_t.reshape(bsz, N_EXPERTS, nb, LANES), cap)
        clo, chi, st, xlo, xhi = _loop_tables(cum[..., 0].astype(jnp.int32), cap, win)
        xe, gs = _gather(clo, chi, x1b.reshape(bsz, seq, d), pos, gate, cap)
        y = _ffn(l, xe, gs, w_gate, w_up, w_down)
        pos_t = jnp.swapaxes(pos.reshape(bsz, N_EXPERTS, seq), 1, 2)
        xo, xbo = _combine(alpha, win, st, xlo, xhi, y, pos_t, x1.reshape(bsz, seq, d), ln2_g[l], ln2_b[l])
        xf, xb = xo.reshape(m, d), xbo.reshape(m, d)
    return xf.reshape(bsz, seq, d)
```

```python
import functools
import math

import jax
import jax.numpy as jnp
from jax import lax
from jax.experimental import pallas as pl
from jax.experimental.pallas import tpu as pltpu

F32 = jnp.float32
BF16 = jnp.bfloat16

D_MODEL = 1024
N_HEADS = 8
N_KV_HEADS = 2
HEAD_DIM = 128
GROUP = N_HEADS // N_KV_HEADS
GRID_W = 64
ROPE_THETA = 10000.0
C_CONV = D_MODEL
CONV_K = 31
CONV_HALO = 16
N_EXPERTS = 16
EC_FACTOR = 2
LN_EPS = 1e-5
RMS_EPS = 1e-6
Q_W = N_HEADS * HEAD_DIM
KV_W = N_KV_HEADS * HEAD_DIM
AUG_DIM = 2 * HEAD_DIM
KV_AUG_W = N_KV_HEADS * AUG_DIM
MAX_FIXED_SHIFT = 60.0

LANES = 128
SUBLANES = 8
VMEM_LIMIT = 56 * 1024 * 1024

TM_LN = 512
TM_PROJ = 512
TM_QKV = 256
TS_CONV = 256
RC_CONV = 32
TQ_ATTN = 512
TK_ATTN = 512
TM_MERGE = 256
TF_FFN = 512
TT_COMB = 512
WIN_COMB = 256
TC_GATHER = 512
NHEAD_GATHER = 3


def _params(*sem):
    return pltpu.CompilerParams(dimension_semantics=sem, vmem_limit_bytes=VMEM_LIMIT)


def _sigmoid(x):
    return 1.0 / (1.0 + jnp.exp(-x))


def _layer_norm(x, g, b):
    mu = jnp.mean(x, axis=-1, keepdims=True)
    xc = x - mu
    var = jnp.mean(xc * xc, axis=-1, keepdims=True)
    return xc * lax.rsqrt(var + LN_EPS) * g + b


def _dot(a, b):
    return jnp.dot(a, b, preferred_element_type=F32)


def _ln0_kernel(x_ref, g_ref, b_ref, xf_ref, xb_ref):
    y = _layer_norm(x_ref[...], g_ref[...], b_ref[...])
    xf_ref[...] = y
    xb_ref[...] = y.astype(BF16)


def _ln0(x2d, g, b):
    m, d = x2d.shape
    row = pl.BlockSpec((TM_LN, d), lambda i: (i, 0))
    vec = pl.BlockSpec((1, d), lambda i: (0, 0))
    return pl.pallas_call(
        _ln0_kernel,
        grid=(m // TM_LN,),
        in_specs=[row, vec, vec],
        out_specs=[row, row],
        out_shape=[jax.ShapeDtypeStruct((m, d), F32), jax.ShapeDtypeStruct((m, d), BF16)],
        compiler_params=_params("parallel"),
        name="ln0",
    )(x2d, g.reshape(1, d), b.reshape(1, d))


def _glu_kernel(h_ref, wv_ref, wg_ref, bv_ref, bg_ref, u_ref):
    h = h_ref[...]
    val = _dot(h, wv_ref[...]) + bv_ref[...]
    gate = _dot(h, wg_ref[...]) + bg_ref[...]
    u_ref[...] = val * _sigmoid(gate)


def _glu(hb, wv, wg, bv, bg):
    m, d = hb.shape
    c = wv.shape[1]
    row_in = pl.BlockSpec((TM_PROJ, d), lambda i: (i, 0))
    w = pl.BlockSpec((d, c), lambda i: (0, 0))
    vec = pl.BlockSpec((1, c), lambda i: (0, 0))
    return pl.pallas_call(
        _glu_kernel,
        grid=(m // TM_PROJ,),
        in_specs=[row_in, w, w, vec, vec],
        out_specs=pl.BlockSpec((TM_PROJ, c), lambda i: (i, 0)),
        out_shape=jax.ShapeDtypeStruct((m, c), F32),
        compiler_params=_params("parallel"),
        name="glu",
    )(hb, wv, wg, bv.reshape(1, c), bg.reshape(1, c))


def _norm_rope(x, g, cos, sin):
    ms = jnp.mean(x * x, axis=-1, keepdims=True)
    xn = x * lax.rsqrt(ms + RMS_EPS) * g
    return xn * cos + pltpu.roll(xn, HEAD_DIM // 2, 1) * sin


def _qkv_kernel(h_ref, wq_ref, wk_ref, wv_ref, bq_ref, bk_ref, bv_ref, gq_ref, gk_ref,
                cq_ref, sq_ref, ck_ref, sk_ref, q_ref, k_ref, v_ref):
    h = h_ref[...]
    zq = _dot(h, wq_ref[...]) + bq_ref[...]
    zk = _dot(h, wk_ref[...]) + bk_ref[...]
    zv = _dot(h, wv_ref[...]) + bv_ref[...]
    cq, sq, ck, sk = cq_ref[...], sq_ref[...], ck_ref[...], sk_ref[...]
    gq, gk = gq_ref[...], gk_ref[...]
    for hd in range(N_HEADS):
        sl = slice(hd * HEAD_DIM, (hd + 1) * HEAD_DIM)
        q_ref[:, sl] = _norm_rope(zq[:, sl], gq, cq, sq).astype(BF16)
    one_col = jnp.where(lax.broadcasted_iota(jnp.int32, (h.shape[0], AUG_DIM - HEAD_DIM), 1) == 0, 1.0, 0.0).astype(BF16)
    for hd in range(N_KV_HEADS):
        sl = slice(hd * HEAD_DIM, (hd + 1) * HEAD_DIM)
        k_ref[:, hd * AUG_DIM:hd * AUG_DIM + HEAD_DIM] = _norm_rope(zk[:, sl], gk, ck, sk).astype(BF16)
        k_ref[:, hd * AUG_DIM + HEAD_DIM:(hd + 1) * AUG_DIM] = one_col
        v_ref[:, hd * AUG_DIM:hd * AUG_DIM + HEAD_DIM] = zv[:, sl].astype(BF16)
        v_ref[:, hd * AUG_DIM + HEAD_DIM:(hd + 1) * AUG_DIM] = one_col


def _qkv(hb, wq, wk, wv, bq, bk, bv, gq, gk, cq, sq, ck, sk, seq):
    m, d = hb.shape
    tm = TM_QKV
    per_seq = seq // tm
    row = lambda w: pl.BlockSpec((tm, w), lambda i: (i, 0))
    full = lambda a: pl.BlockSpec(a.shape, lambda i: (0, 0))
    tab = pl.BlockSpec((tm, HEAD_DIM), lambda i: (i % per_seq, 0))
    args = (hb, wq, wk, wv, bq.reshape(1, -1), bk.reshape(1, -1), bv.reshape(1, -1),
            gq.reshape(1, -1), gk.reshape(1, -1), cq, sq, ck, sk)
    in_specs = [row(d)] + [full(a) for a in args[1:9]] + [tab] * 4
    return pl.pallas_call(
        _qkv_kernel,
        grid=(m // tm,),
        in_specs=in_specs,
        out_specs=[row(Q_W), row(KV_AUG_W), row(KV_AUG_W)],
        out_shape=[jax.ShapeDtypeStruct((m, Q_W), BF16), jax.ShapeDtypeStruct((m, KV_AUG_W), BF16),
                   jax.ShapeDtypeStruct((m, KV_AUG_W), BF16)],
        compiler_params=_params("parallel"),
        name="qkv",
    )(*args)


def _conv_kernel(up_ref, uc_ref, un_ref, w_ref, wb_ref, g_ref, b_ref, o_ref, ext_ref, sh_ref, wrep_ref):
    i = pl.program_id(1)
    n = pl.num_programs(1)
    ts = uc_ref.shape[1]
    ext_ref[0:CONV_HALO, :] = jnp.where(i > 0, up_ref[0], 0.0)
    ext_ref[CONV_HALO:CONV_HALO + ts, :] = uc_ref[0]
    ext_ref[CONV_HALO + ts:, :] = jnp.where(i < n - 1, un_ref[0], 0.0)
    span = sh_ref.shape[1]
    for s in range(SUBLANES):
        sh_ref[s] = ext_ref[s:s + span, :]
    c = uc_ref.shape[2]
    for k in range(CONV_K):
        wrep_ref[k] = jnp.broadcast_to(w_ref[k:k + 1, :], (SUBLANES, c))
    g, b = g_ref[...], b_ref[...]
    first = CONV_HALO - CONV_K // 2
    groups = RC_CONV // SUBLANES
    for r0 in range(0, ts, RC_CONV):
        acc = jnp.zeros((groups, SUBLANES, c), F32) + wb_ref[...]
        for k in range(CONV_K):
            a, s = divmod(first + k, SUBLANES)
            rows = sh_ref[s, r0 + a * SUBLANES:r0 + a * SUBLANES + RC_CONV, :]
            acc = acc + wrep_ref[k] * rows.reshape(groups, SUBLANES, c)
        y = _layer_norm(acc.reshape(RC_CONV, c), g, b)
        o_ref[0, r0:r0 + RC_CONV, :] = (y * _sigmoid(y)).astype(BF16)


def _conv(u, dw, dw_b, ln_g, ln_b):
    bsz, seq, c = u.shape
    ts = TS_CONV
    hb = ts // CONV_HALO
    last = seq // CONV_HALO - 1
    vec = pl.BlockSpec((1, c), lambda b, i: (0, 0))
    return pl.pallas_call(
        _conv_kernel,
        grid=(bsz, seq // ts),
        in_specs=[
            pl.BlockSpec((1, CONV_HALO, c), lambda b, i: (b, jnp.maximum(i * hb - 1, 0), 0)),
            pl.BlockSpec((1, ts, c), lambda b, i: (b, i, 0)),
            pl.BlockSpec((1, CONV_HALO, c), lambda b, i: (b, jnp.minimum((i + 1) * hb, last), 0)),
            pl.BlockSpec((CONV_K, c), lambda b, i: (0, 0)),
            vec, vec, vec,
        ],
        out_specs=pl.BlockSpec((1, ts, c), lambda b, i: (b, i, 0)),
        out_shape=jax.ShapeDtypeStruct((bsz, seq, c), BF16),
        scratch_shapes=[pltpu.VMEM((ts + 2 * CONV_HALO, c), F32),
                        pltpu.VMEM((SUBLANES, ts + 2 * CONV_HALO - SUBLANES, c), F32),
                        pltpu.VMEM((CONV_K, SUBLANES, c), F32)],
        compiler_params=_params("parallel", "parallel"),
        name="conv",
    )(u, u, u, dw, dw_b.reshape(1, c), ln_g.reshape(1, c), ln_b.reshape(1, c))


def _attn_kernel(q_ref, k_ref, v_ref, o_ref, kmax_ref):
    i = pl.program_id(2)
    tq = q_ref.shape[1]
    nk = k_ref.shape[1] // TK_ATTN
    rows = GROUP * tq
    pad = AUG_DIM - HEAD_DIM

    @pl.when(i == 0)
    def _():
        kk = k_ref[0].astype(F32)
        ksq = jnp.sum(kk * kk, axis=1, keepdims=True) - 1.0
        kmax_ref[...] = jnp.max(ksq, axis=0, keepdims=True)

    q = jnp.concatenate([q_ref[0, :, g * HEAD_DIM:(g + 1) * HEAD_DIM] for g in range(GROUP)], axis=0)
    qf = q.astype(F32)
    shift = jnp.sqrt(jnp.sum(qf * qf, axis=1, keepdims=True) * kmax_ref[...])
    first = lax.broadcasted_iota(jnp.int32, (rows, pad), 1) == 0
    nt = (((1,), (1,)), ((), ()))

    def chunk(c):
        start = pl.multiple_of(c * TK_ATTN, TK_ATTN)
        return k_ref[0, pl.ds(start, TK_ATTN), :], v_ref[0, pl.ds(start, TK_ATTN), :]

    def finish(acc):
        out = (acc[:, :HEAD_DIM] / acc[:, HEAD_DIM:HEAD_DIM + 1]).astype(BF16)
        for g in range(GROUP):
            o_ref[0, :, g * HEAD_DIM:(g + 1) * HEAD_DIM] = out[g * tq:(g + 1) * tq]

    def fixed_shift():
        qa = jnp.concatenate([q, jnp.where(first, -shift, 0.0).astype(BF16)], axis=1)

        def body(c, acc):
            k, v = chunk(c)
            s = lax.dot_general(qa, k, nt, preferred_element_type=F32)
            return acc + _dot(jnp.exp2(s).astype(BF16), v)

        finish(lax.fori_loop(0, nk, body, jnp.zeros((rows, AUG_DIM), F32), unroll=True))

    def running_max():
        qa = jnp.concatenate([q, jnp.zeros((rows, pad), BF16)], axis=1)

        def body(c, carry):
            m, acc = carry
            k, v = chunk(c)
            s = lax.dot_general(qa, k, nt, preferred_element_type=F32)
            m_new = jnp.maximum(m, jnp.max(s, axis=-1, keepdims=True))
            acc = jnp.exp2(m - m_new) * acc + _dot(jnp.exp2(s - m_new).astype(BF16), v)
            return m_new, acc

        init = (jnp.full((rows, 1), -jnp.inf, F32), jnp.zeros((rows, AUG_DIM), F32))
        finish(lax.fori_loop(0, nk, body, init)[1])

    lax.cond(jnp.max(shift) <= MAX_FIXED_SHIFT, fixed_shift, running_max)


def _attention(q, k, v):
    bsz, seq, _ = q.shape
    gw = GROUP * HEAD_DIM
    kv = pl.BlockSpec((1, seq, AUG_DIM), lambda b, h, i: (b, 0, h))
    return pl.pallas_call(
        _attn_kernel,
        grid=(bsz, N_KV_HEADS, seq // TQ_ATTN),
        in_specs=[pl.BlockSpec((1, TQ_ATTN, gw), lambda b, h, i: (b, i, h)), kv, kv],
        out_specs=pl.BlockSpec((1, TQ_ATTN, gw), lambda b, h, i: (b, i, h)),
        out_shape=jax.ShapeDtypeStruct((bsz, seq, Q_W), BF16),
        scratch_shapes=[pltpu.VMEM((1, 1), F32)],
        compiler_params=_params("parallel", "parallel", "arbitrary"),
        name="attn",
    )(q, k, v)


def _merge_kernel(alpha, h_ref, x_ref, c_ref, a_ref, wgc_ref, wga_ref, wpw_ref, wo_ref, wout_ref,
                  bgc_ref, bga_ref, bpw_ref, bout_ref, lg_ref, lb_ref, wrh_ref, wrl_ref,
                  x1_ref, x1b_ref, lgt_ref):
    h = h_ref[...]
    gc = _sigmoid(_dot(h, wgc_ref[...]) + bgc_ref[...])
    ga = _sigmoid(_dot(h, wga_ref[...]) + bga_ref[...])
    yc = _dot(c_ref[...], wpw_ref[...]) + bpw_ref[...]
    ya = _dot(a_ref[...], wo_ref[...])
    mix = _dot((gc * yc + ga * ya).astype(BF16), wout_ref[...]) + bout_ref[...]
    x1 = _layer_norm(alpha * x_ref[...] + mix, lg_ref[...], lb_ref[...])
    x1_ref[...] = x1
    hi = x1.astype(BF16)
    lo = (x1 - hi.astype(F32)).astype(BF16)
    x1b_ref[...] = hi
    lgt_ref[...] = _dot(hi, wrh_ref[...]) + _dot(lo, wrh_ref[...]) + _dot(hi, wrl_ref[...])


def _merge(alpha, hb, x, c, a, wgc, wga, wpw, wo, wout, bgc, bga, bpw, bout, lg, lb, wrh, wrl):
    m, d = x.shape
    tm = TM_MERGE
    row = pl.BlockSpec((tm, d), lambda i: (i, 0))
    w = pl.BlockSpec((d, d), lambda i: (0, 0), pipeline_mode=pl.Buffered(1))
    vec = pl.BlockSpec((1, d), lambda i: (0, 0))
    wr = pl.BlockSpec((d, LANES), lambda i: (0, 0))
    vecs = [v.reshape(1, d) for v in (bgc, bga, bpw, bout, lg, lb)]
    return pl.pallas_call(
        functools.partial(_merge_kernel, alpha),
        grid=(m // tm,),
        in_specs=[row] * 4 + [w] * 5 + [vec] * 6 + [wr, wr],
        out_specs=[row, row, pl.BlockSpec((tm, LANES), lambda i: (i, 0))],
        out_shape=[jax.ShapeDtypeStruct((m, d), F32), jax.ShapeDtypeStruct((m, d), BF16),
                   jax.ShapeDtypeStruct((m, LANES), F32)],
        compiler_params=_params("parallel"),
        name="merge",
    )(hb, x, c, a, wgc, wga, wpw, wo, wout, *vecs, wrh, wrl)


def _route_kernel(cap, lg_ref, pos_ref, gate_ref, cum_ref):
    lg = lg_ref[0]
    ne, nb, _ = lg.shape
    rows = ne * nb
    ex = jnp.exp(lg - jnp.max(lg, axis=0, keepdims=True))
    aff = ex / jnp.sum(ex, axis=0, keepdims=True)

    def count(mask):
        x = jnp.where(mask, 1.0, 0.0)
        return jnp.sum(jnp.sum(x, axis=1, keepdims=True), axis=2, keepdims=True)

    def bit_step(i, tb):
        cand = tb | jnp.left_shift(jnp.int32(1), 30 - i)
        ok = count(aff >= lax.bitcast_convert_type(cand, F32)) >= cap
        return jnp.where(ok, cand, tb)

    tb = lax.fori_loop(0, 31, bit_step, jnp.zeros((ne, 1, 1), jnp.int32))
    above = aff >= lax.bitcast_convert_type(tb + 1, F32)
    tied = jnp.logical_and(aff >= lax.bitcast_convert_type(tb, F32), jnp.logical_not(above))

    li = lax.broadcasted_iota(jnp.int32, (LANES, LANES), 0)
    lj = lax.broadcasted_iota(jnp.int32, (LANES, LANES), 1)
    upper = jnp.where(li <= lj, 1.0, 0.0).astype(BF16)
    ones = jnp.ones((LANES, LANES), BF16)
    ri = lax.broadcasted_iota(jnp.int32, (rows, rows), 0)
    rj = lax.broadcasted_iota(jnp.int32, (rows, rows), 1)
    shift = nb.bit_length() - 1
    assert 1 << shift == nb, "token blocks per sequence must be a power of two"
    same = jnp.right_shift(ri, shift) == jnp.right_shift(rj, shift)
    before = jnp.where(jnp.logical_and(same, rj < ri), 1.0, 0.0).astype(BF16)

    def prefix(mask):
        x = jnp.where(mask, 1.0, 0.0).reshape(rows, LANES)
        xb = x.astype(BF16)
        incl = _dot(xb, upper)
        tot = _dot(xb, ones)
        off = _dot(before, tot.astype(BF16))
        return (off + incl - x).reshape(ne, nb, LANES), off.reshape(ne, nb, LANES)

    need = cap - count(above)
    tie_rank, _ = prefix(tied)
    chosen = jnp.logical_or(above, jnp.logical_and(tied, tie_rank < need))
    pos, cum = prefix(chosen)
    pos_ref[0] = jnp.where(chosen, pos, -1.0)
    gate_ref[0] = jnp.where(chosen, aff, 0.0)
    cum_ref[0] = cum


def _route(logits_t, cap):
    bsz, ne, nb, _ = logits_t.shape
    blk = pl.BlockSpec((1, ne, nb, LANES), lambda b: (b, 0, 0, 0))
    shp = jax.ShapeDtypeStruct(logits_t.shape, F32)
    return pl.pallas_call(
        functools.partial(_route_kernel, cap),
        grid=(bsz,),
        in_specs=[blk],
        out_specs=[blk, blk, blk],
        out_shape=[shp, shp, shp],
        compiler_params=_params("parallel"),
        name="route",
    )(logits_t)


def _gather_kernel(clo_ref, chi_ref, x_ref, pos_ref, gate_ref, xe_ref, gs_ref, acc_ref, gacc_ref):
    b = pl.program_id(0)
    e = pl.program_id(1)
    ne = pl.num_programs(1)
    nj = xe_ref.shape[2] // LANES
    per = TC_GATHER // LANES
    nchunk = x_ref.shape[1] // TC_GATHER
    nhead = min(NHEAD_GATHER, nchunk)

    def span(c, n, j):
        rows = range(n * per)
        p = jnp.concatenate([pos_ref[0, 0, pl.ds(c * per + r, 1), :] for r in rows], axis=1)
        gt = jnp.concatenate([gate_ref[0, 0, pl.ds(c * per + r, 1), :] for r in rows], axis=1)
        slot = lax.broadcasted_iota(jnp.int32, (LANES, n * TC_GATHER), 0).astype(F32) + float(j * LANES)
        match = slot == p
        xk = x_ref[0, pl.ds(pl.multiple_of(c * TC_GATHER, TC_GATHER), n * TC_GATHER), :]
        return (_dot(jnp.where(match, 1.0, 0.0).astype(BF16), xk),
                jnp.sum(jnp.where(match, gt, 0.0), axis=1, keepdims=True))

    for j in range(nj):
        t = (b * ne + e) * nj + j
        c0 = jnp.minimum(clo_ref[t], nchunk - nhead)
        acc_ref[...], gacc_ref[...] = span(c0, nhead, j)

        def body(c, carry):
            d, g = span(c, 1, j)
            acc_ref[...] += d
            gacc_ref[...] += g
            return carry

        lax.fori_loop(c0 + nhead, chi_ref[t], body, 0)
        xe_ref[0, 0, j * LANES:(j + 1) * LANES, :] = acc_ref[...].astype(BF16)
        gs_ref[0, 0, j * LANES:(j + 1) * LANES, :] = gacc_ref[...]


def _gather(clo, chi, x1b, pos, gate, cap):
    bsz, seq, d = x1b.shape
    ne, nb = pos.shape[1], pos.shape[2]
    sel = pl.BlockSpec((1, 1, nb, LANES), lambda b, e, *_: (b, e, 0, 0))
    grid_spec = pltpu.PrefetchScalarGridSpec(
        num_scalar_prefetch=2,
        grid=(bsz, ne),
        in_specs=[pl.BlockSpec((1, seq, d), lambda b, e, *_: (b, 0, 0)), sel, sel],
        out_specs=[pl.BlockSpec((1, 1, cap, d), lambda b, e, *_: (b, e, 0, 0)),
                   pl.BlockSpec((1, 1, cap, 1), lambda b, e, *_: (b, e, 0, 0))],
        scratch_shapes=[pltpu.VMEM((LANES, d), F32), pltpu.VMEM((LANES, 1), F32)],
    )
    return pl.pallas_call(
        _gather_kernel,
        grid_spec=grid_spec,
        out_shape=[jax.ShapeDtypeStruct((bsz, ne, cap, d), BF16), jax.ShapeDtypeStruct((bsz, ne, cap, 1), F32)],
        compiler_params=_params("parallel", "parallel"),
        name="gather",
    )(clo, chi, x1b, pos, gate)


def _ffn_kernel(xe_ref, gs_ref, wg_ref, wu_ref, wd_ref, y_ref, acc_ref):
    f = pl.program_id(1)
    nf = pl.num_programs(1)
    bsz = xe_ref.shape[0]
    wg = wg_ref[0, 0].astype(BF16)
    wu = wu_ref[0, 0].astype(BF16)
    wd = wd_ref[0, 0].astype(BF16)

    for bi in range(bsz):
        x = xe_ref[bi, 0]
        a = _dot(x, wg)
        u = _dot(x, wu)
        hid = (a * _sigmoid(a) * u).astype(BF16)
        acc_ref[bi] = jnp.where(f == 0, 0.0, acc_ref[bi]) + _dot(hid, wd)

    @pl.when(f == nf - 1)
    def _():
        y_ref[:, 0] = (acc_ref[...] * gs_ref[:, 0]).astype(BF16)


def _ffn(layer, xe, gs, w_gate, w_up, w_down):
    bsz, ne, cap, d = xe.shape
    ff = w_gate.shape[3]
    tf = TF_FFN
    return pl.pallas_call(
        _ffn_kernel,
        grid=(ne, ff // tf),
        in_specs=[
            pl.BlockSpec((bsz, 1, cap, d), lambda e, f: (0, e, 0, 0)),
            pl.BlockSpec((bsz, 1, cap, 1), lambda e, f: (0, e, 0, 0)),
            pl.BlockSpec((1, 1, d, tf), lambda e, f: (layer, e, 0, f)),
            pl.BlockSpec((1, 1, d, tf), lambda e, f: (layer, e, 0, f)),
            pl.BlockSpec((1, 1, tf, d), lambda e, f: (layer, e, f, 0)),
        ],
        out_specs=pl.BlockSpec((bsz, 1, cap, d), lambda e, f: (0, e, 0, 0)),
        out_shape=jax.ShapeDtypeStruct((bsz, ne, cap, d), BF16),
        scratch_shapes=[pltpu.VMEM((bsz, cap, d), F32)],
        compiler_params=_params("parallel", "arbitrary"),
        name="ffn",
    )(xe, gs, w_gate, w_up, w_down)


def _combine_kernel(alpha, win, st_ref, xlo_ref, xhi_ref, y_ref, post_ref, x1_ref, g_ref, b_ref,
                    xo_ref, xbo_ref, moe_ref):
    b = pl.program_id(0)
    i = pl.program_id(1)
    ni = pl.num_programs(1)
    ne = y_ref.shape[1]
    tt = x1_ref.shape[1]
    lane_w = lax.broadcasted_iota(jnp.int32, (tt, win), 1).astype(F32)
    lane = lax.broadcasted_iota(jnp.int32, (tt, LANES), 1).astype(F32)
    moe = jnp.zeros((tt, x1_ref.shape[2]), F32)
    for e in range(ne):
        st = pl.multiple_of(st_ref[(b * ne + e) * ni + i], LANES)
        rel = post_ref[0, :, e:e + 1] - st.astype(F32)
        onehot = jnp.where(rel == lane_w, 1.0, 0.0).astype(BF16)
        moe = moe + _dot(onehot, y_ref[0, e, pl.ds(st, win), :])
    moe_ref[...] = moe
    for e in range(ne):
        pcol = post_ref[0, :, e:e + 1]
        t = (b * ne + e) * ni + i

        def body(j, carry):
            onehot = jnp.where(pcol - (j * LANES).astype(F32) == lane, 1.0, 0.0).astype(BF16)
            yk = y_ref[0, e, pl.ds(pl.multiple_of(j * LANES, LANES), LANES), :]
            moe_ref[...] += _dot(onehot, yk)
            return carry

        lax.fori_loop(xlo_ref[t], xhi_ref[t], body, 0)
    y = _layer_norm(alpha * x1_ref[0] + moe_ref[...], g_ref[...], b_ref[...])
    xo_ref[0] = y
    xbo_ref[0] = y.astype(BF16)


def _combine(alpha, win, st, xlo, xhi, y, pos_t, x1, g, b):
    bsz, seq, d = x1.shape
    ne, cap = y.shape[1], y.shape[2]
    tt = TT_COMB
    tok = pl.BlockSpec((1, tt, d), lambda bb, i, *_: (bb, i, 0))
    sel = pl.BlockSpec((1, tt, ne), lambda bb, i, *_: (bb, i, 0))
    vec = pl.BlockSpec((1, d), lambda bb, i, *_: (0, 0))
    y_spec = pl.BlockSpec((1, ne, cap, d), lambda bb, i, *_: (bb, 0, 0, 0), pipeline_mode=pl.Buffered(1))
    grid_spec = pltpu.PrefetchScalarGridSpec(
        num_scalar_prefetch=3,
        grid=(bsz, seq // tt),
        in_specs=[y_spec, sel, tok, vec, vec],
        out_specs=[tok, tok],
        scratch_shapes=[pltpu.VMEM((tt, d), F32)],
    )
    return pl.pallas_call(
        functools.partial(_combine_kernel, alpha, win),
        grid_spec=grid_spec,
        out_shape=[jax.ShapeDtypeStruct((bsz, seq, d), F32), jax.ShapeDtypeStruct((bsz, seq, d), BF16)],
        compiler_params=_params("parallel", "arbitrary"),
        name="combine",
    )(st, xlo, xhi, y, pos_t, x1, g.reshape(1, d), b.reshape(1, d))


def _rope_tables(seq):
    rows = seq // GRID_W
    row = jnp.repeat(jnp.arange(rows, dtype=jnp.int32), GRID_W).astype(F32)
    col = jnp.tile(jnp.arange(GRID_W, dtype=jnp.int32), rows).astype(F32)
    axis_dim = HEAD_DIM // 2
    freqs = 1.0 / (ROPE_THETA ** (jnp.arange(0, axis_dim, 2, dtype=F32) / axis_dim))
    ang = jnp.concatenate([row[:, None] * freqs[None], col[:, None] * freqs[None]], axis=-1)
    cos, sin = jnp.cos(ang), jnp.sin(ang)
    return jnp.concatenate([cos, cos], axis=-1), jnp.concatenate([-sin, sin], axis=-1)


def _deinterleave_heads(w, n_heads):
    lead = w.shape[:-1]
    w = w.reshape(lead + (n_heads, HEAD_DIM // 2, 2))
    return jnp.swapaxes(w, -1, -2).reshape(lead + (n_heads * HEAD_DIM,))


def _loop_tables(cum, cap, win):
    bsz, ne, nb = cum.shape
    cum_ext = jnp.concatenate([cum, jnp.full((bsz, ne, 1), cap, jnp.int32)], axis=-1)
    nj = cap // LANES
    slot0 = (jnp.arange(nj, dtype=jnp.int32) * LANES)[None, None, :, None]
    klo = jnp.sum((cum_ext[:, :, None, 1:] <= slot0).astype(jnp.int32), axis=-1)
    khi = jnp.sum((cum_ext[:, :, None, :nb] < slot0 + LANES).astype(jnp.int32), axis=-1)
    per = TC_GATHER // LANES
    clo = klo // per
    chi = (khi + per - 1) // per
    per = TT_COMB // LANES
    s0 = cum_ext[:, :, 0:nb:per]
    s1 = cum_ext[:, :, per::per]
    st = jnp.minimum((s0 // LANES) * LANES, cap - win)
    xlo = (st + win) // LANES
    xhi = jnp.where(s1 > s0, (s1 - 1) // LANES + 1, 0)
    return clo.reshape(-1), chi.reshape(-1), st.reshape(-1), xlo.reshape(-1), xhi.reshape(-1)


def kernel(x, ln0_g, ln0_b, w_in, b_in, conv_dw, conv_dw_b, conv_ln_g, conv_ln_b, conv_pw_w, conv_pw_b, q_norm_g, k_norm_g, w_o, w_out, b_out, ln1_g, ln1_b, w_router, w_gate, w_up, w_down, ln2_g, ln2_b):
    bsz, seq, d = x.shape
    depth = w_in.shape[0]
    alpha = (2.0 * depth) ** 0.25
    cap = EC_FACTOR * seq // N_EXPERTS
    win = min(WIN_COMB, cap)
    m = bsz * seq
    nb = seq // LANES

    off_q = 2 * C_CONV
    off_k = off_q + Q_W
    off_v = off_k + KV_W
    off_gc = off_v + KV_W
    off_ga = off_gc + d

    cos, sin = _rope_tables(seq)
    scale = math.log2(math.e) / math.sqrt(HEAD_DIM)
    cq, sq = cos * scale, sin * scale
    gq = _deinterleave_heads(q_norm_g, 1)
    gk = _deinterleave_heads(k_norm_g, 1)

    w_in_b = w_in.astype(BF16)
    w_pw_b = conv_pw_w.astype(BF16)
    w_o_b = w_o.astype(BF16)
    w_out_b = w_out.astype(BF16)
    w_r = jnp.pad(w_router, ((0, 0), (0, 0), (0, LANES - N_EXPERTS)))
    w_r_hi = w_r.astype(BF16)
    w_r_lo = (w_r - w_r_hi.astype(F32)).astype(BF16)

    xf, xb = _ln0(x.reshape(m, d), ln0_g, ln0_b)
    for l in range(depth):
        wl, bl = w_in_b[l], b_in[l]
        u = _glu(xb, wl[:, :C_CONV], wl[:, C_CONV:off_q], bl[:C_CONV], bl[C_CONV:off_q])
        q, k, v = _qkv(
            xb, _deinterleave_heads(wl[:, off_q:off_k], N_HEADS), _deinterleave_heads(wl[:, off_k:off_v], N_KV_HEADS),
            wl[:, off_v:off_gc], _deinterleave_heads(bl[off_q:off_k], N_HEADS),
            _deinterleave_heads(bl[off_k:off_v], N_KV_HEADS), bl[off_v:off_gc],
            gq[l], gk[l], cq, sq, cos, sin, seq)
        c = _conv(u.reshape(bsz, seq, C_CONV), conv_dw[l], conv_dw_b[l], conv_ln_g[l], conv_ln_b[l])
        a = _attention(q.reshape(bsz, seq, Q_W), k.reshape(bsz, seq, KV_AUG_W), v.reshape(bsz, seq, KV_AUG_W))
        x1, x1b, logits = _merge(
            alpha, xb, xf, c.reshape(m, d), a.reshape(m, d),
            wl[:, off_gc:off_ga], wl[:, off_ga:], w_pw_b[l], w_o_b[l], w_out_b[l],
            bl[off_gc:off_ga], bl[off_ga:], conv_pw_b[l], b_out[l], ln1_g[l], ln1_b[l], w_r_hi[l], w_r_lo[l])
        logits_t = jnp.swapaxes(logits[:, :N_EXPERTS].reshape(bsz, seq, N_EXPERTS), 1, 2)
        pos, gate, cum = _route(logits_t.reshape(bsz, N_EXPERTS, nb, LANES), cap)
        clo, chi, st, xlo, xhi = _loop_tables(cum[..., 0].astype(jnp.int32), cap, win)
        xe, gs = _gather(clo, chi, x1b.reshape(bsz, seq, d), pos, gate, cap)
        y = _ffn(l, xe, gs, w_gate, w_up, w_down)
        pos_t = jnp.swapaxes(pos.reshape(bsz, N_EXPERTS, seq), 1, 2)
        xo, xbo = _combine(alpha, win, st, xlo, xhi, y, pos_t, x1.reshape(bsz, seq, d), ln2_g[l], ln2_b[l])
        xf, xb = xo.reshape(m, d), xbo.reshape(m, d)
    return xf.reshape(bsz, seq, d)
```

```python
import functools
import math

import jax
import jax.numpy as jnp
from jax import lax
from jax.experimental import pallas as pl
from jax.experimental.pallas import tpu as pltpu

F32 = jnp.float32
BF16 = jnp.bfloat16

D_MODEL = 1024
N_HEADS = 8
N_KV_HEADS = 2
HEAD_DIM = 128
GROUP = N_HEADS // N_KV_HEADS
GRID_W = 64
ROPE_THETA = 10000.0
C_CONV = D_MODEL
CONV_K = 31
CONV_HALO = 16
N_EXPERTS = 16
EC_FACTOR = 2
LN_EPS = 1e-5
RMS_EPS = 1e-6
Q_W = N_HEADS * HEAD_DIM
KV_W = N_KV_HEADS * HEAD_DIM
AUG_DIM = 2 * HEAD_DIM
KV_AUG_W = N_KV_HEADS * AUG_DIM
VT_ROWS = HEAD_DIM + 16
MAX_FIXED_SHIFT = 60.0

LANES = 128
SUBLANES = 8
VMEM_LIMIT = 56 * 1024 * 1024

TM_LN = 512
TM_PROJ = 512
TM_QKV = 256
TS_CONV = 256
RC_CONV = 32
TQ_ATTN = 1024
TK_ATTN = 512
TM_MERGE = 256
TF_FFN = 512
TT_COMB = 512
WIN_COMB = 256
TC_GATHER = 512
NHEAD_GATHER = 3


def _params(*sem):
    return pltpu.CompilerParams(dimension_semantics=sem, vmem_limit_bytes=VMEM_LIMIT)


def _sigmoid(x):
    return 1.0 / (1.0 + jnp.exp(-x))


def _layer_norm(x, g, b):
    mu = jnp.mean(x, axis=-1, keepdims=True)
    xc = x - mu
    var = jnp.mean(xc * xc, axis=-1, keepdims=True)
    return xc * lax.rsqrt(var + LN_EPS) * g + b


def _dot(a, b):
    return jnp.dot(a, b, preferred_element_type=F32)


def _ln0_kernel(x_ref, g_ref, b_ref, xf_ref, xb_ref):
    y = _layer_norm(x_ref[...], g_ref[...], b_ref[...])
    xf_ref[...] = y
    xb_ref[...] = y.astype(BF16)


def _ln0(x2d, g, b):
    m, d = x2d.shape
    row = pl.BlockSpec((TM_LN, d), lambda i: (i, 0))
    vec = pl.BlockSpec((1, d), lambda i: (0, 0))
    return pl.pallas_call(
        _ln0_kernel,
        grid=(m // TM_LN,),
        in_specs=[row, vec, vec],
        out_specs=[row, row],
        out_shape=[jax.ShapeDtypeStruct((m, d), F32), jax.ShapeDtypeStruct((m, d), BF16)],
        compiler_params=_params("parallel"),
        name="ln0",
    )(x2d, g.reshape(1, d), b.reshape(1, d))


def _glu_kernel(h_ref, wv_ref, wg_ref, bv_ref, bg_ref, u_ref):
    h = h_ref[...]
    val = _dot(h, wv_ref[...]) + bv_ref[...]
    gate = _dot(h, wg_ref[...]) + bg_ref[...]
    u_ref[...] = val * _sigmoid(gate)


def _glu(hb, wv, wg, bv, bg):
    m, d = hb.shape
    c = wv.shape[1]
    row_in = pl.BlockSpec((TM_PROJ, d), lambda i: (i, 0))
    w = pl.BlockSpec((d, c), lambda i: (0, 0))
    vec = pl.BlockSpec((1, c), lambda i: (0, 0))
    return pl.pallas_call(
        _glu_kernel,
        grid=(m // TM_PROJ,),
        in_specs=[row_in, w, w, vec, vec],
        out_specs=pl.BlockSpec((TM_PROJ, c), lambda i: (i, 0)),
        out_shape=jax.ShapeDtypeStruct((m, c), F32),
        compiler_params=_params("parallel"),
        name="glu",
    )(hb, wv, wg, bv.reshape(1, c), bg.reshape(1, c))


def _norm_rope(x, g, cos, sin):
    ms = jnp.mean(x * x, axis=-1, keepdims=True)
    xn = x * lax.rsqrt(ms + RMS_EPS) * g
    return xn * cos + pltpu.roll(xn, HEAD_DIM // 2, 1) * sin


def _qkv_kernel(h_ref, wq_ref, wk_ref, wv_ref, bq_ref, bk_ref, bv_ref, gq_ref, gk_ref,
                cq_ref, sq_ref, ck_ref, sk_ref, q_ref, k_ref, vt_ref):
    h = h_ref[...]
    zq = _dot(h, wq_ref[...]) + bq_ref[...]
    zk = _dot(h, wk_ref[...]) + bk_ref[...]
    zv = _dot(h, wv_ref[...]) + bv_ref[...]
    cq, sq, ck, sk = cq_ref[...], sq_ref[...], ck_ref[...], sk_ref[...]
    gq, gk = gq_ref[...], gk_ref[...]
    for hd in range(N_HEADS):
        sl = slice(hd * HEAD_DIM, (hd + 1) * HEAD_DIM)
        q_ref[:, sl] = _norm_rope(zq[:, sl], gq, cq, sq).astype(BF16)
    tm = h.shape[0]
    one_col = jnp.where(lax.broadcasted_iota(jnp.int32, (tm, AUG_DIM - HEAD_DIM), 1) == 0, 1.0, 0.0).astype(BF16)
    one_row = jnp.where(lax.broadcasted_iota(jnp.int32, (VT_ROWS - HEAD_DIM, tm), 0) == 0, 1.0, 0.0).astype(BF16)
    for hd in range(N_KV_HEADS):
        sl = slice(hd * HEAD_DIM, (hd + 1) * HEAD_DIM)
        k_ref[:, hd * AUG_DIM:hd * AUG_DIM + HEAD_DIM] = _norm_rope(zk[:, sl], gk, ck, sk).astype(BF16)
        k_ref[:, hd * AUG_DIM + HEAD_DIM:(hd + 1) * AUG_DIM] = one_col
        vt_ref[0, hd * VT_ROWS:hd * VT_ROWS + HEAD_DIM, :] = zv[:, sl].T.astype(BF16)
        vt_ref[0, hd * VT_ROWS + HEAD_DIM:(hd + 1) * VT_ROWS, :] = one_row


def _qkv(hb, wq, wk, wv, bq, bk, bv, gq, gk, cq, sq, ck, sk, seq):
    m, d = hb.shape
    tm = TM_QKV
    per_seq = seq // tm
    row = lambda w: pl.BlockSpec((tm, w), lambda i: (i, 0))
    full = lambda a: pl.BlockSpec(a.shape, lambda i: (0, 0))
    tab = pl.BlockSpec((tm, HEAD_DIM), lambda i: (i % per_seq, 0))
    args = (hb, wq, wk, wv, bq.reshape(1, -1), bk.reshape(1, -1), bv.reshape(1, -1),
            gq.reshape(1, -1), gk.reshape(1, -1), cq, sq, ck, sk)
    in_specs = [row(d)] + [full(a) for a in args[1:9]] + [tab] * 4
    return pl.pallas_call(
        _qkv_kernel,
        grid=(m // tm,),
        in_specs=in_specs,
        out_specs=[row(Q_W), row(KV_AUG_W),
                   pl.BlockSpec((1, N_KV_HEADS * VT_ROWS, tm), lambda i: (i // per_seq, 0, i % per_seq))],
        out_shape=[jax.ShapeDtypeStruct((m, Q_W), BF16), jax.ShapeDtypeStruct((m, KV_AUG_W), BF16),
                   jax.ShapeDtypeStruct((m // seq, N_KV_HEADS * VT_ROWS, seq), BF16)],
        compiler_params=_params("parallel"),
        name="qkv",
    )(*args)


def _conv_kernel(up_ref, uc_ref, un_ref, w_ref, wb_ref, g_ref, b_ref, o_ref, ext_ref, sh_ref, wrep_ref):
    i = pl.program_id(1)
    n = pl.num_programs(1)
    ts = uc_ref.shape[1]
    ext_ref[0:CONV_HALO, :] = jnp.where(i > 0, up_ref[0], 0.0)
    ext_ref[CONV_HALO:CONV_HALO + ts, :] = uc_ref[0]
    ext_ref[CONV_HALO + ts:, :] = jnp.where(i < n - 1, un_ref[0], 0.0)
    span = sh_ref.shape[1]
    for s in range(SUBLANES):
        sh_ref[s] = ext_ref[s:s + span, :]
    c = uc_ref.shape[2]
    for k in range(CONV_K):
        wrep_ref[k] = jnp.broadcast_to(w_ref[k:k + 1, :], (SUBLANES, c))
    g, b = g_ref[...], b_ref[...]
    first = CONV_HALO - CONV_K // 2
    groups = RC_CONV // SUBLANES
    for r0 in range(0, ts, RC_CONV):
        acc = jnp.zeros((groups, SUBLANES, c), F32) + wb_ref[...]
        for k in range(CONV_K):
            a, s = divmod(first + k, SUBLANES)
            rows = sh_ref[s, r0 + a * SUBLANES:r0 + a * SUBLANES + RC_CONV, :]
            acc = acc + wrep_ref[k] * rows.reshape(groups, SUBLANES, c)
        y = _layer_norm(acc.reshape(RC_CONV, c), g, b)
        o_ref[0, r0:r0 + RC_CONV, :] = (y * _sigmoid(y)).astype(BF16)


def _conv(u, dw, dw_b, ln_g, ln_b):
    bsz, seq, c = u.shape
    ts = TS_CONV
    hb = ts // CONV_HALO
    last = seq // CONV_HALO - 1
    vec = pl.BlockSpec((1, c), lambda b, i: (0, 0))
    return pl.pallas_call(
        _conv_kernel,
        grid=(bsz, seq // ts),
        in_specs=[
            pl.BlockSpec((1, CONV_HALO, c), lambda b, i: (b, jnp.maximum(i * hb - 1, 0), 0)),
            pl.BlockSpec((1, ts, c), lambda b, i: (b, i, 0)),
            pl.BlockSpec((1, CONV_HALO, c), lambda b, i: (b, jnp.minimum((i + 1) * hb, last), 0)),
            pl.BlockSpec((CONV_K, c), lambda b, i: (0, 0)),
            vec, vec, vec,
        ],
        out_specs=pl.BlockSpec((1, ts, c), lambda b, i: (b, i, 0)),
        out_shape=jax.ShapeDtypeStruct((bsz, seq, c), BF16),
        scratch_shapes=[pltpu.VMEM((ts + 2 * CONV_HALO, c), F32),
                        pltpu.VMEM((SUBLANES, ts + 2 * CONV_HALO - SUBLANES, c), F32),
                        pltpu.VMEM((CONV_K, SUBLANES, c), F32)],
        compiler_params=_params("parallel", "parallel"),
        name="conv",
    )(u, u, u, dw, dw_b.reshape(1, c), ln_g.reshape(1, c), ln_b.reshape(1, c))


def _attn_kernel(q_ref, k_ref, vt_ref, o_ref, kmax_ref):
    i = pl.program_id(2)
    tq = q_ref.shape[1]
    nk = k_ref.shape[1] // TK_ATTN
    rows = GROUP * tq
    pad = AUG_DIM - HEAD_DIM

    @pl.when(i == 0)
    def _():
        kk = k_ref[0].astype(F32)
        ksq = jnp.sum(kk * kk, axis=1, keepdims=True) - 1.0
        kmax_ref[...] = jnp.max(ksq, axis=0, keepdims=True)

    q = jnp.concatenate([q_ref[0, :, g * HEAD_DIM:(g + 1) * HEAD_DIM] for g in range(GROUP)], axis=0)
    qf = q.astype(F32)
    shift = jnp.sqrt(jnp.sum(qf * qf, axis=1, keepdims=True) * kmax_ref[...])
    first = lax.broadcasted_iota(jnp.int32, (rows, pad), 1) == 0
    nt = (((1,), (1,)), ((), ()))

    def chunk(c):
        start = pl.multiple_of(c * TK_ATTN, TK_ATTN)
        return k_ref[0, pl.ds(start, TK_ATTN), :], vt_ref[0, :, pl.ds(start, TK_ATTN)]

    def finish(acc):
        out = acc[:HEAD_DIM] / acc[HEAD_DIM:HEAD_DIM + 1]
        for g in range(GROUP):
            o_ref[0, :, g * HEAD_DIM:(g + 1) * HEAD_DIM] = out[:, g * tq:(g + 1) * tq].T.astype(BF16)

    def fixed_shift():
        qa = jnp.concatenate([q, jnp.where(first, -shift, 0.0).astype(BF16)], axis=1)

        def body(c, acc):
            k, vt = chunk(c)
            st = lax.dot_general(k, qa, nt, preferred_element_type=F32)
            return acc + _dot(vt, jnp.exp2(st).astype(BF16))

        finish(lax.fori_loop(0, nk, body, jnp.zeros((VT_ROWS, rows), F32), unroll=True))

    def running_max():
        qa = jnp.concatenate([q, jnp.zeros((rows, pad), BF16)], axis=1)

        def body(c, carry):
            m, acc = carry
            k, vt = chunk(c)
            st = lax.dot_general(k, qa, nt, preferred_element_type=F32)
            m_new = jnp.maximum(m, jnp.max(st, axis=0, keepdims=True))
            acc = jnp.exp2(m - m_new) * acc + _dot(vt, jnp.exp2(st - m_new).astype(BF16))
            return m_new, acc

        init = (jnp.full((1, rows), -jnp.inf, F32), jnp.zeros((VT_ROWS, rows), F32))
        finish(lax.fori_loop(0, nk, body, init)[1])

    lax.cond(jnp.max(shift) <= MAX_FIXED_SHIFT, fixed_shift, running_max)


def _attention(q, k, vt):
    bsz, seq, _ = q.shape
    gw = GROUP * HEAD_DIM
    return pl.pallas_call(
        _attn_kernel,
        grid=(bsz, N_KV_HEADS, seq // TQ_ATTN),
        in_specs=[pl.BlockSpec((1, TQ_ATTN, gw), lambda b, h, i: (b, i, h)),
                  pl.BlockSpec((1, seq, AUG_DIM), lambda b, h, i: (b, 0, h)),
                  pl.BlockSpec((1, VT_ROWS, seq), lambda b, h, i: (b, h, 0))],
        out_specs=pl.BlockSpec((1, TQ_ATTN, gw), lambda b, h, i: (b, i, h)),
        out_shape=jax.ShapeDtypeStruct((bsz, seq, Q_W), BF16),
        scratch_shapes=[pltpu.VMEM((1, 1), F32)],
        compiler_params=_params("parallel", "parallel", "arbitrary"),
        name="attn",
    )(q, k, vt)


def _merge_kernel(alpha, h_ref, x_ref, c_ref, a_ref, wgc_ref, wga_ref, wpw_ref, wo_ref, wout_ref,
                  bgc_ref, bga_ref, bpw_ref, bout_ref, lg_ref, lb_ref, wrh_ref, wrl_ref,
                  x1_ref, x1b_ref, lgt_ref):
    h = h_ref[...]
    gc = _sigmoid(_dot(h, wgc_ref[...]) + bgc_ref[...])
    ga = _sigmoid(_dot(h, wga_ref[...]) + bga_ref[...])
    yc = _dot(c_ref[...], wpw_ref[...]) + bpw_ref[...]
    ya = _dot(a_ref[...], wo_ref[...])
    mix = _dot((gc * yc + ga * ya).astype(BF16), wout_ref[...]) + bout_ref[...]
    x1 = _layer_norm(alpha * x_ref[...] + mix, lg_ref[...], lb_ref[...])
    x1_ref[...] = x1
    hi = x1.astype(BF16)
    lo = (x1 - hi.astype(F32)).astype(BF16)
    x1b_ref[...] = hi
    lgt_ref[...] = _dot(hi, wrh_ref[...]) + _dot(lo, wrh_ref[...]) + _dot(hi, wrl_ref[...])


def _merge(alpha, hb, x, c, a, wgc, wga, wpw, wo, wout, bgc, bga, bpw, bout, lg, lb, wrh, wrl):
    m, d = x.shape
    tm = TM_MERGE
    row = pl.BlockSpec((tm, d), lambda i: (i, 0))
    w = pl.BlockSpec((d, d), lambda i: (0, 0), pipeline_mode=pl.Buffered(1))
    vec = pl.BlockSpec((1, d), lambda i: (0, 0))
    wr = pl.BlockSpec((d, LANES), lambda i: (0, 0))
    vecs = [v.reshape(1, d) for v in (bgc, bga, bpw, bout, lg, lb)]
    return pl.pallas_call(
        functools.partial(_merge_kernel, alpha),
        grid=(m // tm,),
        in_specs=[row] * 4 + [w] * 5 + [vec] * 6 + [wr, wr],
        out_specs=[row, row, pl.BlockSpec((tm, LANES), lambda i: (i, 0))],
        out_shape=[jax.ShapeDtypeStruct((m, d), F32), jax.ShapeDtypeStruct((m, d), BF16),
                   jax.ShapeDtypeStruct((m, LANES), F32)],
        compiler_params=_params("parallel"),
        name="merge",
    )(hb, x, c, a, wgc, wga, wpw, wo, wout, *vecs, wrh, wrl)


def _route_kernel(cap, lg_ref, pos_ref, gate_ref, cum_ref):
    lg = lg_ref[0]
    ne, nb, _ = lg.shape
    rows = ne * nb
    ex = jnp.exp(lg - jnp.max(lg, axis=0, keepdims=True))
    aff = ex / jnp.sum(ex, axis=0, keepdims=True)

    def count(mask):
        x = jnp.where(mask, 1.0, 0.0)
        return jnp.sum(jnp.sum(x, axis=1, keepdims=True), axis=2, keepdims=True)

    def bit_step(i, tb):
        cand = tb | jnp.left_shift(jnp.int32(1), 30 - i)
        ok = count(aff >= lax.bitcast_convert_type(cand, F32)) >= cap
        return jnp.where(ok, cand, tb)

    tb = lax.fori_loop(0, 31, bit_step, jnp.zeros((ne, 1, 1), jnp.int32))
    above = aff >= lax.bitcast_convert_type(tb + 1, F32)
    tied = jnp.logical_and(aff >= lax.bitcast_convert_type(tb, F32), jnp.logical_not(above))

    li = lax.broadcasted_iota(jnp.int32, (LANES, LANES), 0)
    lj = lax.broadcasted_iota(jnp.int32, (LANES, LANES), 1)
    upper = jnp.where(li <= lj, 1.0, 0.0).astype(BF16)
    ones = jnp.ones((LANES, LANES), BF16)
    ri = lax.broadcasted_iota(jnp.int32, (rows, rows), 0)
    rj = lax.broadcasted_iota(jnp.int32, (rows, rows), 1)
    shift = nb.bit_length() - 1
    assert 1 << shift == nb, "token blocks per sequence must be a power of two"
    same = jnp.right_shift(ri, shift) == jnp.right_shift(rj, shift)
    before = jnp.where(jnp.logical_and(same, rj < ri), 1.0, 0.0).astype(BF16)

    def prefix(mask):
        x = jnp.where(mask, 1.0, 0.0).reshape(rows, LANES)
        xb = x.astype(BF16)
        incl = _dot(xb, upper)
        tot = _dot(xb, ones)
        off = _dot(before, tot.astype(BF16))
        return (off + incl - x).reshape(ne, nb, LANES), off.reshape(ne, nb, LANES)

    need = cap - count(above)
    tie_rank, _ = prefix(tied)
    chosen = jnp.logical_or(above, jnp.logical_and(tied, tie_rank < need))
    pos, cum = prefix(chosen)
    pos_ref[0] = jnp.where(chosen, pos, -1.0)
    gate_ref[0] = jnp.where(chosen, aff, 0.0)
    cum_ref[0] = cum


def _route(logits_t, cap):
    bsz, ne, nb, _ = logits_t.shape
    blk = pl.BlockSpec((1, ne, nb, LANES), lambda b: (b, 0, 0, 0))
    shp = jax.ShapeDtypeStruct(logits_t.shape, F32)
    return pl.pallas_call(
        functools.partial(_route_kernel, cap),
        grid=(bsz,),
        in_specs=[blk],
        out_specs=[blk, blk, blk],
        out_shape=[shp, shp, shp],
        compiler_params=_params("parallel"),
        name="route",
    )(logits_t)


def _gather_kernel(clo_ref, chi_ref, x_ref, pos_ref, gate_ref, xe_ref, gs_ref, acc_ref, gacc_ref):
    b = pl.program_id(0)
    e = pl.program_id(1)
    ne = pl.num_programs(1)
    nj = xe_ref.shape[2] // LANES
    per = TC_GATHER // LANES
    nchunk = x_ref.shape[1] // TC_GATHER
    nhead = min(NHEAD_GATHER, nchunk)

    def span(c, n, j):
        rows = range(n * per)
        p = jnp.concatenate([pos_ref[0, 0, pl.ds(c * per + r, 1), :] for r in rows], axis=1)
        gt = jnp.concatenate([gate_ref[0, 0, pl.ds(c * per + r, 1), :] for r in rows], axis=1)
        slot = lax.broadcasted_iota(jnp.int32, (LANES, n * TC_GATHER), 0).astype(F32) + float(j * LANES)
        match = slot == p
        xk = x_ref[0, pl.ds(pl.multiple_of(c * TC_GATHER, TC_GATHER), n * TC_GATHER), :]
        return (_dot(jnp.where(match, 1.0, 0.0).astype(BF16), xk),
                jnp.sum(jnp.where(match, gt, 0.0), axis=1, keepdims=True))

    for j in range(nj):
        t = (b * ne + e) * nj + j
        c0 = jnp.minimum(clo_ref[t], nchunk - nhead)
        acc_ref[...], gacc_ref[...] = span(c0, nhead, j)

        def body(c, carry):
            d, g = span(c, 1, j)
            acc_ref[...] += d
            gacc_ref[...] += g
            return carry

        lax.fori_loop(c0 + nhead, chi_ref[t], body, 0)
        xe_ref[0, 0, j * LANES:(j + 1) * LANES, :] = acc_ref[...].astype(BF16)
        gs_ref[0, 0, j * LANES:(j + 1) * LANES, :] = gacc_ref[...]


def _gather(clo, chi, x1b, pos, gate, cap):
    bsz, seq, d = x1b.shape
    ne, nb = pos.shape[1], pos.shape[2]
    sel = pl.BlockSpec((1, 1, nb, LANES), lambda b, e, *_: (b, e, 0, 0))
    grid_spec = pltpu.PrefetchScalarGridSpec(
        num_scalar_prefetch=2,
        grid=(bsz, ne),
        in_specs=[pl.BlockSpec((1, seq, d), lambda b, e, *_: (b, 0, 0)), sel, sel],
        out_specs=[pl.BlockSpec((1, 1, cap, d), lambda b, e, *_: (b, e, 0, 0)),
                   pl.BlockSpec((1, 1, cap, 1), lambda b, e, *_: (b, e, 0, 0))],
        scratch_shapes=[pltpu.VMEM((LANES, d), F32), pltpu.VMEM((LANES, 1), F32)],
    )
    return pl.pallas_call(
        _gather_kernel,
        grid_spec=grid_spec,
        out_shape=[jax.ShapeDtypeStruct((bsz, ne, cap, d), BF16), jax.ShapeDtypeStruct((bsz, ne, cap, 1), F32)],
        compiler_params=_params("parallel", "parallel"),
        name="gather",
    )(clo, chi, x1b, pos, gate)


def _ffn_kernel(xe_ref, gs_ref, wg_ref, wu_ref, wd_ref, y_ref, acc_ref):
    f = pl.program_id(1)
    nf = pl.num_programs(1)
    bsz = xe_ref.shape[0]
    wg = wg_ref[0, 0].astype(BF16)
    wu = wu_ref[0, 0].astype(BF16)
    wd = wd_ref[0, 0].astype(BF16)

    for bi in range(bsz):
        x = xe_ref[bi, 0]
        a = _dot(x, wg)
        u = _dot(x, wu)
        hid = (a * _sigmoid(a) * u).astype(BF16)
        acc_ref[bi] = jnp.where(f == 0, 0.0, acc_ref[bi]) + _dot(hid, wd)

    @pl.when(f == nf - 1)
    def _():
        y_ref[:, 0] = (acc_ref[...] * gs_ref[:, 0]).astype(BF16)


def _ffn(layer, xe, gs, w_gate, w_up, w_down):
    bsz, ne, cap, d = xe.shape
    ff = w_gate.shape[3]
    tf = TF_FFN
    return pl.pallas_call(
        _ffn_kernel,
        grid=(ne, ff // tf),
        in_specs=[
            pl.BlockSpec((bsz, 1, cap, d), lambda e, f: (0, e, 0, 0)),
            pl.BlockSpec((bsz, 1, cap, 1), lambda e, f: (0, e, 0, 0)),
            pl.BlockSpec((1, 1, d, tf), lambda e, f: (layer, e, 0, f)),
            pl.BlockSpec((1, 1, d, tf), lambda e, f: (layer, e, 0, f)),
            pl.BlockSpec((1, 1, tf, d), lambda e, f: (layer, e, f, 0)),
        ],
        out_specs=pl.BlockSpec((bsz, 1, cap, d), lambda e, f: (0, e, 0, 0)),
        out_shape=jax.ShapeDtypeStruct((bsz, ne, cap, d), BF16),
        scratch_shapes=[pltpu.VMEM((bsz, cap, d), F32)],
        compiler_params=_params("parallel", "arbitrary"),
        name="ffn",
    )(xe, gs, w_gate, w_up, w_down)


def _combine_kernel(alpha, win, st_ref, xlo_ref, xhi_ref, y_ref, post_ref, x1_ref, g_ref, b_ref,
                    xo_ref, xbo_ref, moe_ref):
    b = pl.program_id(0)
    i = pl.program_id(1)
    ni = pl.num_programs(1)
    ne = y_ref.shape[1]
    tt = x1_ref.shape[1]
    lane_w = lax.broadcasted_iota(jnp.int32, (tt, win), 1).astype(F32)
    lane = lax.broadcasted_iota(jnp.int32, (tt, LANES), 1).astype(F32)
    moe = jnp.zeros((tt, x1_ref.shape[2]), F32)
    for e in range(ne):
        st = pl.multiple_of(st_ref[(b * ne + e) * ni + i], LANES)
        rel = post_ref[0, :, e:e + 1] - st.astype(F32)
        onehot = jnp.where(rel == lane_w, 1.0, 0.0).astype(BF16)
        moe = moe + _dot(onehot, y_ref[0, e, pl.ds(st, win), :])
    moe_ref[...] = moe
    for e in range(ne):
        pcol = post_ref[0, :, e:e + 1]
        t = (b * ne + e) * ni + i

        def body(j, carry):
            onehot = jnp.where(pcol - (j * LANES).astype(F32) == lane, 1.0, 0.0).astype(BF16)
            yk = y_ref[0, e, pl.ds(pl.multiple_of(j * LANES, LANES), LANES), :]
            moe_ref[...] += _dot(onehot, yk)
            return carry

        lax.fori_loop(xlo_ref[t], xhi_ref[t], body, 0)
    y = _layer_norm(alpha * x1_ref[0] + moe_ref[...], g_ref[...], b_ref[...])
    xo_ref[0] = y
    xbo_ref[0] = y.astype(BF16)


def _combine(alpha, win, st, xlo, xhi, y, pos_t, x1, g, b):
    bsz, seq, d = x1.shape
    ne, cap = y.shape[1], y.shape[2]
    tt = TT_COMB
    tok = pl.BlockSpec((1, tt, d), lambda bb, i, *_: (bb, i, 0))
    sel = pl.BlockSpec((1, tt, ne), lambda bb, i, *_: (bb, i, 0))
    vec = pl.BlockSpec((1, d), lambda bb, i, *_: (0, 0))
    y_spec = pl.BlockSpec((1, ne, cap, d), lambda bb, i, *_: (bb, 0, 0, 0), pipeline_mode=pl.Buffered(1))
    grid_spec = pltpu.PrefetchScalarGridSpec(
        num_scalar_prefetch=3,
        grid=(bsz, seq // tt),
        in_specs=[y_spec, sel, tok, vec, vec],
        out_specs=[tok, tok],
        scratch_shapes=[pltpu.VMEM((tt, d), F32)],
    )
    return pl.pallas_call(
        functools.partial(_combine_kernel, alpha, win),
        grid_spec=grid_spec,
        out_shape=[jax.ShapeDtypeStruct((bsz, seq, d), F32), jax.ShapeDtypeStruct((bsz, seq, d), BF16)],
        compiler_params=_params("parallel", "arbitrary"),
        name="combine",
    )(st, xlo, xhi, y, pos_t, x1, g.reshape(1, d), b.reshape(1, d))


def _rope_tables(seq):
    rows = seq // GRID_W
    row = jnp.repeat(jnp.arange(rows, dtype=jnp.int32), GRID_W).astype(F32)
    col = jnp.tile(jnp.arange(GRID_W, dtype=jnp.int32), rows).astype(F32)
    axis_dim = HEAD_DIM // 2
    freqs = 1.0 / (ROPE_THETA ** (jnp.arange(0, axis_dim, 2, dtype=F32) / axis_dim))
    ang = jnp.concatenate([row[:, None] * freqs[None], col[:, None] * freqs[None]], axis=-1)
    cos, sin = jnp.cos(ang), jnp.sin(ang)
    return jnp.concatenate([cos, cos], axis=-1), jnp.concatenate([-sin, sin], axis=-1)


def _deinterleave_heads(w, n_heads):
    lead = w.shape[:-1]
    w = w.reshape(lead + (n_heads, HEAD_DIM // 2, 2))
    return jnp.swapaxes(w, -1, -2).reshape(lead + (n_heads * HEAD_DIM,))


def _loop_tables(cum, cap, win):
    bsz, ne, nb = cum.shape
    cum_ext = jnp.concatenate([cum, jnp.full((bsz, ne, 1), cap, jnp.int32)], axis=-1)
    nj = cap // LANES
    slot0 = (jnp.arange(nj, dtype=jnp.int32) * LANES)[None, None, :, None]
    klo = jnp.sum((cum_ext[:, :, None, 1:] <= slot0).astype(jnp.int32), axis=-1)
    khi = jnp.sum((cum_ext[:, :, None, :nb] < slot0 + LANES).astype(jnp.int32), axis=-1)
    per = TC_GATHER // LANES
    clo = klo // per
    chi = (khi + per - 1) // per
    per = TT_COMB // LANES
    s0 = cum_ext[:, :, 0:nb:per]
    s1 = cum_ext[:, :, per::per]
    st = jnp.minimum((s0 // LANES) * LANES, cap - win)
    xlo = (st + win) // LANES
    xhi = jnp.where(s1 > s0, (s1 - 1) // LANES + 1, 0)
    return clo.reshape(-1), chi.reshape(-1), st.reshape(-1), xlo.reshape(-1), xhi.reshape(-1)


def kernel(x, ln0_g, ln0_b, w_in, b_in, conv_dw, conv_dw_b, conv_ln_g, conv_ln_b, conv_pw_w, conv_pw_b, q_norm_g, k_norm_g, w_o, w_out, b_out, ln1_g, ln1_b, w_router, w_gate, w_up, w_down, ln2_g, ln2_b):
    bsz, seq, d = x.shape
    depth = w_in.shape[0]
    alpha = (2.0 * depth) ** 0.25
    cap = EC_FACTOR * seq // N_EXPERTS
    win = min(WIN_COMB, cap)
    m = bsz * seq
    nb = seq // LANES

    off_q = 2 * C_CONV
    off_k = off_q + Q_W
    off_v = off_k + KV_W
    off_gc = off_v + KV_W
    off_ga = off_gc + d

    cos, sin = _rope_tables(seq)
    scale = math.log2(math.e) / math.sqrt(HEAD_DIM)
    cq, sq = cos * scale, sin * scale
    gq = _deinterleave_heads(q_norm_g, 1)
    gk = _deinterleave_heads(k_norm_g, 1)

    w_in_b = w_in.astype(BF16)
    w_pw_b = conv_pw_w.astype(BF16)
    w_o_b = w_o.astype(BF16)
    w_out_b = w_out.astype(BF16)
    w_r = jnp.pad(w_router, ((0, 0), (0, 0), (0, LANES - N_EXPERTS)))
    w_r_hi = w_r.astype(BF16)
    w_r_lo = (w_r - w_r_hi.astype(F32)).astype(BF16)

    xf, xb = _ln0(x.reshape(m, d), ln0_g, ln0_b)
    for l in range(depth):
        wl, bl = w_in_b[l], b_in[l]
        u = _glu(xb, wl[:, :C_CONV], wl[:, C_CONV:off_q], bl[:C_CONV], bl[C_CONV:off_q])
        q, k, vt = _qkv(
            xb, _deinterleave_heads(wl[:, off_q:off_k], N_HEADS), _deinterleave_heads(wl[:, off_k:off_v], N_KV_HEADS),
            wl[:, off_v:off_gc], _deinterleave_heads(bl[off_q:off_k], N_HEADS),
            _deinterleave_heads(bl[off_k:off_v], N_KV_HEADS), bl[off_v:off_gc],
            gq[l], gk[l], cq, sq, cos, sin, seq)
        c = _conv(u.reshape(bsz, seq, C_CONV), conv_dw[l], conv_dw_b[l], conv_ln_g[l], conv_ln_b[l])
        a = _attention(q.reshape(bsz, seq, Q_W), k.reshape(bsz, seq, KV_AUG_W), vt)
        x1, x1b, logits = _merge(
            alpha, xb, xf, c.reshape(m, d), a.reshape(m, d),
            wl[:, off_gc:off_ga], wl[:, off_ga:], w_pw_b[l], w_o_b[l], w_out_b[l],
            bl[off_gc:off_ga], bl[off_ga:], conv_pw_b[l], b_out[l], ln1_g[l], ln1_b[l], w_r_hi[l], w_r_lo[l])
        logits_t = jnp.swapaxes(logits[:, :N_EXPERTS].reshape(bsz, seq, N_EXPERTS), 1, 2)
        pos, gate, cum = _route(logits_t.reshape(bsz, N_EXPERTS, nb, LANES), cap)
        clo, chi, st, xlo, xhi = _loop_tables(cum[..., 0].astype(jnp.int32), cap, win)
        xe, gs = _gather(clo, chi, x1b.reshape(bsz, seq, d), pos, gate, cap)
        y = _ffn(l, xe, gs, w_gate, w_up, w_down)
        pos_t = jnp.swapaxes(pos.reshape(bsz, N_EXPERTS, seq), 1, 2)
        xo, xbo = _combine(alpha, win, st, xlo, xhi, y, pos_t, x1.reshape(bsz, seq, d), ln2_g[l], ln2_b[l])
        xf, xb = xo.reshape(m, d), xbo.reshape(m, d)
    return xf.reshape(bsz, seq, d)
```

```python
import functools
import math

import jax
import jax.numpy as jnp
from jax import lax
from jax.experimental import pallas as pl
from jax.experimental.pallas import tpu as pltpu

F32 = jnp.float32
BF16 = jnp.bfloat16

D_MODEL = 1024
N_HEADS = 8
N_KV_HEADS = 2
HEAD_DIM = 128
GROUP = N_HEADS // N_KV_HEADS
GRID_W = 64
ROPE_THETA = 10000.0
C_CONV = D_MODEL
CONV_K = 31
CONV_HALO = 16
N_EXPERTS = 16
EC_FACTOR = 2
LN_EPS = 1e-5
RMS_EPS = 1e-6
Q_W = N_HEADS * HEAD_DIM
KV_W = N_KV_HEADS * HEAD_DIM
AUG_DIM = 2 * HEAD_DIM
KV_AUG_W = N_KV_HEADS * AUG_DIM
VT_ROWS = HEAD_DIM + 16
MAX_FIXED_SHIFT = 60.0

LANES = 128
SUBLANES = 8
BF16_ROWS = 16
VMEM_LIMIT = 56 * 1024 * 1024

TM_LN = 512
TM_PROJ = 512
TM_QKV = 256
TS_CONV = 256
RC_CONV = 32
TQ_ATTN = 1024
TK_ATTN = 512
TM_MERGE = 256
TF_FFN = 512
TT_COMB = 512
WIN_COMB = 256
TC_GATHER = 256
NHEAD_GATHER = 5


def _params(*sem):
    return pltpu.CompilerParams(dimension_semantics=sem, vmem_limit_bytes=VMEM_LIMIT)


def _sigmoid(x):
    return 1.0 / (1.0 + jnp.exp(-x))


def _layer_norm(x, g, b):
    mu = jnp.mean(x, axis=-1, keepdims=True)
    xc = x - mu
    var = jnp.mean(xc * xc, axis=-1, keepdims=True)
    return xc * lax.rsqrt(var + LN_EPS) * g + b


def _dot(a, b):
    return jnp.dot(a, b, preferred_element_type=F32)


def _ln0_kernel(x_ref, g_ref, b_ref, xf_ref, xb_ref):
    y = _layer_norm(x_ref[...], g_ref[...], b_ref[...])
    xf_ref[...] = y
    xb_ref[...] = y.astype(BF16)


def _ln0(x2d, g, b):
    m, d = x2d.shape
    row = pl.BlockSpec((TM_LN, d), lambda i: (i, 0))
    vec = pl.BlockSpec((1, d), lambda i: (0, 0))
    return pl.pallas_call(
        _ln0_kernel,
        grid=(m // TM_LN,),
        in_specs=[row, vec, vec],
        out_specs=[row, row],
        out_shape=[jax.ShapeDtypeStruct((m, d), F32), jax.ShapeDtypeStruct((m, d), BF16)],
        compiler_params=_params("parallel"),
        name="ln0",
    )(x2d, g.reshape(1, d), b.reshape(1, d))


def _glu_kernel(h_ref, wv_ref, wg_ref, bv_ref, bg_ref, u_ref):
    h = h_ref[...]
    val = _dot(h, wv_ref[...]) + bv_ref[...]
    gate = _dot(h, wg_ref[...]) + bg_ref[...]
    u_ref[...] = val * _sigmoid(gate)


def _glu(hb, wv, wg, bv, bg):
    m, d = hb.shape
    c = wv.shape[1]
    row_in = pl.BlockSpec((TM_PROJ, d), lambda i: (i, 0))
    w = pl.BlockSpec((d, c), lambda i: (0, 0))
    vec = pl.BlockSpec((1, c), lambda i: (0, 0))
    return pl.pallas_call(
        _glu_kernel,
        grid=(m // TM_PROJ,),
        in_specs=[row_in, w, w, vec, vec],
        out_specs=pl.BlockSpec((TM_PROJ, c), lambda i: (i, 0)),
        out_shape=jax.ShapeDtypeStruct((m, c), F32),
        compiler_params=_params("parallel"),
        name="glu",
    )(hb, wv, wg, bv.reshape(1, c), bg.reshape(1, c))


def _norm_rope(x, g, cos, sin):
    ms = jnp.mean(x * x, axis=-1, keepdims=True)
    xn = x * lax.rsqrt(ms + RMS_EPS) * g
    return xn * cos + pltpu.roll(xn, HEAD_DIM // 2, 1) * sin


def _qkv_kernel(h_ref, wq_ref, wk_ref, wv_ref, bq_ref, bk_ref, bv_ref, gq_ref, gk_ref,
                cq_ref, sq_ref, ck_ref, sk_ref, q_ref, k_ref, vt_ref):
    h = h_ref[...]
    zq = _dot(h, wq_ref[...]) + bq_ref[...]
    zk = _dot(h, wk_ref[...]) + bk_ref[...]
    zv = _dot(h, wv_ref[...]) + bv_ref[...]
    cq, sq, ck, sk = cq_ref[...], sq_ref[...], ck_ref[...], sk_ref[...]
    gq, gk = gq_ref[...], gk_ref[...]
    for hd in range(N_HEADS):
        sl = slice(hd * HEAD_DIM, (hd + 1) * HEAD_DIM)
        q_ref[:, sl] = _norm_rope(zq[:, sl], gq, cq, sq).astype(BF16)
    tm = h.shape[0]
    one_col = jnp.where(lax.broadcasted_iota(jnp.int32, (tm, AUG_DIM - HEAD_DIM), 1) == 0, 1.0, 0.0).astype(BF16)
    one_row = jnp.where(lax.broadcasted_iota(jnp.int32, (VT_ROWS - HEAD_DIM, tm), 0) == 0, 1.0, 0.0).astype(BF16)
    for hd in range(N_KV_HEADS):
        sl = slice(hd * HEAD_DIM, (hd + 1) * HEAD_DIM)
        k_ref[:, hd * AUG_DIM:hd * AUG_DIM + HEAD_DIM] = _norm_rope(zk[:, sl], gk, ck, sk).astype(BF16)
        k_ref[:, hd * AUG_DIM + HEAD_DIM:(hd + 1) * AUG_DIM] = one_col
        vt_ref[0, hd * VT_ROWS:hd * VT_ROWS + HEAD_DIM, :] = zv[:, sl].T.astype(BF16)
        vt_ref[0, hd * VT_ROWS + HEAD_DIM:(hd + 1) * VT_ROWS, :] = one_row


def _qkv(hb, wq, wk, wv, bq, bk, bv, gq, gk, cq, sq, ck, sk, seq):
    m, d = hb.shape
    tm = TM_QKV
    per_seq = seq // tm
    row = lambda w: pl.BlockSpec((tm, w), lambda i: (i, 0))
    full = lambda a: pl.BlockSpec(a.shape, lambda i: (0, 0))
    tab = pl.BlockSpec((tm, HEAD_DIM), lambda i: (i % per_seq, 0))
    args = (hb, wq, wk, wv, bq.reshape(1, -1), bk.reshape(1, -1), bv.reshape(1, -1),
            gq.reshape(1, -1), gk.reshape(1, -1), cq, sq, ck, sk)
    in_specs = [row(d)] + [full(a) for a in args[1:9]] + [tab] * 4
    return pl.pallas_call(
        _qkv_kernel,
        grid=(m // tm,),
        in_specs=in_specs,
        out_specs=[row(Q_W), row(KV_AUG_W),
                   pl.BlockSpec((1, N_KV_HEADS * VT_ROWS, tm), lambda i: (i // per_seq, 0, i % per_seq))],
        out_shape=[jax.ShapeDtypeStruct((m, Q_W), BF16), jax.ShapeDtypeStruct((m, KV_AUG_W), BF16),
                   jax.ShapeDtypeStruct((m // seq, N_KV_HEADS * VT_ROWS, seq), BF16)],
        compiler_params=_params("parallel"),
        name="qkv",
    )(*args)


def _conv_kernel(up_ref, uc_ref, un_ref, w_ref, wb_ref, g_ref, b_ref, o_ref, ext_ref, sh_ref, wrep_ref):
    i = pl.program_id(1)
    n = pl.num_programs(1)
    ts = uc_ref.shape[1]
    ext_ref[0:CONV_HALO, :] = jnp.where(i > 0, up_ref[0], 0.0)
    ext_ref[CONV_HALO:CONV_HALO + ts, :] = uc_ref[0]
    ext_ref[CONV_HALO + ts:, :] = jnp.where(i < n - 1, un_ref[0], 0.0)
    span = sh_ref.shape[1]
    for s in range(SUBLANES):
        sh_ref[s] = ext_ref[s:s + span, :]
    c = uc_ref.shape[2]
    for k in range(CONV_K):
        wrep_ref[k] = jnp.broadcast_to(w_ref[k:k + 1, :], (SUBLANES, c))
    g, b = g_ref[...], b_ref[...]
    first = CONV_HALO - CONV_K // 2
    groups = RC_CONV // SUBLANES
    for r0 in range(0, ts, RC_CONV):
        acc = jnp.zeros((groups, SUBLANES, c), F32) + wb_ref[...]
        for k in range(CONV_K):
            a, s = divmod(first + k, SUBLANES)
            rows = sh_ref[s, r0 + a * SUBLANES:r0 + a * SUBLANES + RC_CONV, :]
            acc = acc + wrep_ref[k] * rows.reshape(groups, SUBLANES, c)
        y = _layer_norm(acc.reshape(RC_CONV, c), g, b)
        o_ref[0, r0:r0 + RC_CONV, :] = (y * _sigmoid(y)).astype(BF16)


def _conv(u, dw, dw_b, ln_g, ln_b):
    bsz, seq, c = u.shape
    ts = TS_CONV
    hb = ts // CONV_HALO
    last = seq // CONV_HALO - 1
    vec = pl.BlockSpec((1, c), lambda b, i: (0, 0))
    return pl.pallas_call(
        _conv_kernel,
        grid=(bsz, seq // ts),
        in_specs=[
            pl.BlockSpec((1, CONV_HALO, c), lambda b, i: (b, jnp.maximum(i * hb - 1, 0), 0)),
            pl.BlockSpec((1, ts, c), lambda b, i: (b, i, 0)),
            pl.BlockSpec((1, CONV_HALO, c), lambda b, i: (b, jnp.minimum((i + 1) * hb, last), 0)),
            pl.BlockSpec((CONV_K, c), lambda b, i: (0, 0)),
            vec, vec, vec,
        ],
        out_specs=pl.BlockSpec((1, ts, c), lambda b, i: (b, i, 0)),
        out_shape=jax.ShapeDtypeStruct((bsz, seq, c), BF16),
        scratch_shapes=[pltpu.VMEM((ts + 2 * CONV_HALO, c), F32),
                        pltpu.VMEM((SUBLANES, ts + 2 * CONV_HALO - SUBLANES, c), F32),
                        pltpu.VMEM((CONV_K, SUBLANES, c), F32)],
        compiler_params=_params("parallel", "parallel"),
        name="conv",
    )(u, u, u, dw, dw_b.reshape(1, c), ln_g.reshape(1, c), ln_b.reshape(1, c))


def _attn_kernel(q_ref, k_ref, vt_ref, o_ref, kmax_ref):
    i = pl.program_id(2)
    tq = q_ref.shape[1]
    nk = k_ref.shape[1] // TK_ATTN
    rows = GROUP * tq
    pad = AUG_DIM - HEAD_DIM

    @pl.when(i == 0)
    def _():
        kk = k_ref[0].astype(F32)
        ksq = jnp.sum(kk * kk, axis=1, keepdims=True) - 1.0
        kmax_ref[...] = jnp.max(ksq, axis=0, keepdims=True)

    q = jnp.concatenate([q_ref[0, :, g * HEAD_DIM:(g + 1) * HEAD_DIM] for g in range(GROUP)], axis=0)
    qf = q.astype(F32)
    shift = jnp.sqrt(jnp.sum(qf * qf, axis=1, keepdims=True) * kmax_ref[...])
    first = lax.broadcasted_iota(jnp.int32, (rows, pad), 1) == 0
    nt = (((1,), (1,)), ((), ()))

    def chunk(c):
        start = pl.multiple_of(c * TK_ATTN, TK_ATTN)
        return k_ref[0, pl.ds(start, TK_ATTN), :], vt_ref[0, :, pl.ds(start, TK_ATTN)]

    def finish(acc):
        out = acc[:HEAD_DIM] / acc[HEAD_DIM:HEAD_DIM + 1]
        for g in range(GROUP):
            o_ref[0, :, g * HEAD_DIM:(g + 1) * HEAD_DIM] = out[:, g * tq:(g + 1) * tq].T.astype(BF16)

    def fixed_shift():
        qa = jnp.concatenate([q, jnp.where(first, -shift, 0.0).astype(BF16)], axis=1)

        def body(c, acc):
            k, vt = chunk(c)
            st = lax.dot_general(k, qa, nt, preferred_element_type=F32)
            return acc + _dot(vt, jnp.exp2(st).astype(BF16))

        finish(lax.fori_loop(0, nk, body, jnp.zeros((VT_ROWS, rows), F32), unroll=True))

    def running_max():
        qa = jnp.concatenate([q, jnp.zeros((rows, pad), BF16)], axis=1)

        def body(c, carry):
            m, acc = carry
            k, vt = chunk(c)
            st = lax.dot_general(k, qa, nt, preferred_element_type=F32)
            m_new = jnp.maximum(m, jnp.max(st, axis=0, keepdims=True))
            acc = jnp.exp2(m - m_new) * acc + _dot(vt, jnp.exp2(st - m_new).astype(BF16))
            return m_new, acc

        init = (jnp.full((1, rows), -jnp.inf, F32), jnp.zeros((VT_ROWS, rows), F32))
        finish(lax.fori_loop(0, nk, body, init)[1])

    lax.cond(jnp.max(shift) <= MAX_FIXED_SHIFT, fixed_shift, running_max)


def _attention(q, k, vt):
    bsz, seq, _ = q.shape
    gw = GROUP * HEAD_DIM
    return pl.pallas_call(
        _attn_kernel,
        grid=(bsz, N_KV_HEADS, seq // TQ_ATTN),
        in_specs=[pl.BlockSpec((1, TQ_ATTN, gw), lambda b, h, i: (b, i, h)),
                  pl.BlockSpec((1, seq, AUG_DIM), lambda b, h, i: (b, 0, h)),
                  pl.BlockSpec((1, VT_ROWS, seq), lambda b, h, i: (b, h, 0))],
        out_specs=pl.BlockSpec((1, TQ_ATTN, gw), lambda b, h, i: (b, i, h)),
        out_shape=jax.ShapeDtypeStruct((bsz, seq, Q_W), BF16),
        scratch_shapes=[pltpu.VMEM((1, 1), F32)],
        compiler_params=_params("parallel", "parallel", "arbitrary"),
        name="attn",
    )(q, k, vt)


def _merge_kernel(alpha, h_ref, x_ref, c_ref, a_ref, wgc_ref, wga_ref, wpw_ref, wo_ref, wout_ref,
                  bgc_ref, bga_ref, bpw_ref, bout_ref, lg_ref, lb_ref, wrh_ref, wrl_ref,
                  x1_ref, x1b_ref, lgt_ref):
    h = h_ref[...]
    gc = _sigmoid(_dot(h, wgc_ref[...]) + bgc_ref[...])
    ga = _sigmoid(_dot(h, wga_ref[...]) + bga_ref[...])
    yc = _dot(c_ref[...], wpw_ref[...]) + bpw_ref[...]
    ya = _dot(a_ref[...], wo_ref[...])
    mix = _dot((gc * yc + ga * ya).astype(BF16), wout_ref[...]) + bout_ref[...]
    x1 = _layer_norm(alpha * x_ref[...] + mix, lg_ref[...], lb_ref[...])
    x1_ref[...] = x1
    hi = x1.astype(BF16)
    lo = (x1 - hi.astype(F32)).astype(BF16)
    x1b_ref[...] = hi
    lgt_ref[...] = _dot(hi, wrh_ref[...]) + _dot(lo, wrh_ref[...]) + _dot(hi, wrl_ref[...])


def _merge(alpha, hb, x, c, a, wgc, wga, wpw, wo, wout, bgc, bga, bpw, bout, lg, lb, wrh, wrl):
    m, d = x.shape
    tm = TM_MERGE
    row = pl.BlockSpec((tm, d), lambda i: (i, 0))
    w = pl.BlockSpec((d, d), lambda i: (0, 0), pipeline_mode=pl.Buffered(1))
    vec = pl.BlockSpec((1, d), lambda i: (0, 0))
    wr = pl.BlockSpec((d, LANES), lambda i: (0, 0))
    vecs = [v.reshape(1, d) for v in (bgc, bga, bpw, bout, lg, lb)]
    return pl.pallas_call(
        functools.partial(_merge_kernel, alpha),
        grid=(m // tm,),
        in_specs=[row] * 4 + [w] * 5 + [vec] * 6 + [wr, wr],
        out_specs=[row, row, pl.BlockSpec((tm, LANES), lambda i: (i, 0))],
        out_shape=[jax.ShapeDtypeStruct((m, d), F32), jax.ShapeDtypeStruct((m, d), BF16),
                   jax.ShapeDtypeStruct((m, LANES), F32)],
        compiler_params=_params("parallel"),
        name="merge",
    )(hb, x, c, a, wgc, wga, wpw, wo, wout, *vecs, wrh, wrl)


def _route_kernel(cap, lg_ref, pos_ref, gate_ref, cum_ref):
    lg = lg_ref[0]
    ne, nb, _ = lg.shape
    rows = ne * nb
    ex = jnp.exp(lg - jnp.max(lg, axis=0, keepdims=True))
    aff = ex / jnp.sum(ex, axis=0, keepdims=True)

    def count(mask):
        x = jnp.where(mask, 1.0, 0.0)
        return jnp.sum(jnp.sum(x, axis=1, keepdims=True), axis=2, keepdims=True)

    def bit_step(i, tb):
        cand = tb | jnp.left_shift(jnp.int32(1), 30 - i)
        ok = count(aff >= lax.bitcast_convert_type(cand, F32)) >= cap
        return jnp.where(ok, cand, tb)

    tb = lax.fori_loop(0, 31, bit_step, jnp.zeros((ne, 1, 1), jnp.int32))
    above = aff >= lax.bitcast_convert_type(tb + 1, F32)
    tied = jnp.logical_and(aff >= lax.bitcast_convert_type(tb, F32), jnp.logical_not(above))

    li = lax.broadcasted_iota(jnp.int32, (LANES, LANES), 0)
    lj = lax.broadcasted_iota(jnp.int32, (LANES, LANES), 1)
    upper = jnp.where(li <= lj, 1.0, 0.0).astype(BF16)
    ones = jnp.ones((LANES, LANES), BF16)
    ri = lax.broadcasted_iota(jnp.int32, (rows, rows), 0)
    rj = lax.broadcasted_iota(jnp.int32, (rows, rows), 1)
    shift = nb.bit_length() - 1
    assert 1 << shift == nb, "token blocks per sequence must be a power of two"
    same = jnp.right_shift(ri, shift) == jnp.right_shift(rj, shift)
    before = jnp.where(jnp.logical_and(same, rj < ri), 1.0, 0.0).astype(BF16)

    def prefix(mask):
        x = jnp.where(mask, 1.0, 0.0).reshape(rows, LANES)
        xb = x.astype(BF16)
        incl = _dot(xb, upper)
        tot = _dot(xb, ones)
        off = _dot(before, tot.astype(BF16))
        return (off + incl - x).reshape(ne, nb, LANES), off.reshape(ne, nb, LANES)

    need = cap - count(above)
    tie_rank, _ = prefix(tied)
    chosen = jnp.logical_or(above, jnp.logical_and(tied, tie_rank < need))
    pos, cum = prefix(chosen)
    pos_ref[0] = jnp.where(chosen, pos, -1.0)
    gate_ref[0] = jnp.where(chosen, aff, 0.0)
    cum_ref[0] = cum


def _route(logits_t, cap):
    bsz, ne, nb, _ = logits_t.shape
    blk = pl.BlockSpec((1, ne, nb, LANES), lambda b: (b, 0, 0, 0))
    shp = jax.ShapeDtypeStruct(logits_t.shape, F32)
    return pl.pallas_call(
        functools.partial(_route_kernel, cap),
        grid=(bsz,),
        in_specs=[blk],
        out_specs=[blk, blk, blk],
        out_shape=[shp, shp, shp],
        compiler_params=_params("parallel"),
        name="route",
    )(logits_t)


def _gather_kernel(clo_ref, chi_ref, x_ref, pos_ref, gate_ref, xe_ref, gs_ref, acc_ref, gacc_ref):
    b = pl.program_id(0)
    e = pl.program_id(1)
    ne = pl.num_programs(1)
    nj = xe_ref.shape[2] // LANES
    per = TC_GATHER // LANES
    nchunk = x_ref.shape[1] // TC_GATHER
    nhead = min(NHEAD_GATHER, nchunk)

    def span(c, n, j):
        rows = range(n * per)
        p = jnp.concatenate([pos_ref[0, 0, pl.ds(c * per + r, 1), :] for r in rows], axis=1)
        gt = jnp.concatenate([gate_ref[0, 0, pl.ds(c * per + r, 1), :] for r in rows], axis=1)
        slot = lax.broadcasted_iota(jnp.int32, (LANES, n * TC_GATHER), 0).astype(F32) + float(j * LANES)
        match = slot == p
        xk = x_ref[0, pl.ds(pl.multiple_of(c * TC_GATHER, TC_GATHER), n * TC_GATHER), :]
        return (_dot(jnp.where(match, 1.0, 0.0).astype(BF16), xk),
                jnp.sum(jnp.where(match, gt, 0.0), axis=1, keepdims=True))

    for j in range(nj):
        t = (b * ne + e) * nj + j
        c0 = jnp.minimum(clo_ref[t], nchunk - nhead)
        acc_ref[...], gacc_ref[...] = span(c0, nhead, j)

        def body(c, carry):
            d, g = span(c, 1, j)
            acc_ref[...] += d
            gacc_ref[...] += g
            return carry

        lax.fori_loop(c0 + nhead, chi_ref[t], body, 0)
        xe_ref[0, 0, j * LANES:(j + 1) * LANES, :] = acc_ref[...].astype(BF16)
        gs_ref[0, 0, j * LANES:(j + 1) * LANES, :] = gacc_ref[...]


def _gather(clo, chi, x1b, pos, gate, cap):
    bsz, seq, d = x1b.shape
    ne, nb = pos.shape[1], pos.shape[2]
    sel = pl.BlockSpec((1, 1, nb, LANES), lambda b, e, *_: (b, e, 0, 0))
    grid_spec = pltpu.PrefetchScalarGridSpec(
        num_scalar_prefetch=2,
        grid=(bsz, ne),
        in_specs=[pl.BlockSpec((1, seq, d), lambda b, e, *_: (b, 0, 0)), sel, sel],
        out_specs=[pl.BlockSpec((1, 1, cap, d), lambda b, e, *_: (b, e, 0, 0)),
                   pl.BlockSpec((1, 1, cap, 1), lambda b, e, *_: (b, e, 0, 0))],
        scratch_shapes=[pltpu.VMEM((LANES, d), F32), pltpu.VMEM((LANES, 1), F32)],
    )
    return pl.pallas_call(
        _gather_kernel,
        grid_spec=grid_spec,
        out_shape=[jax.ShapeDtypeStruct((bsz, ne, cap, d), BF16), jax.ShapeDtypeStruct((bsz, ne, cap, 1), F32)],
        compiler_params=_params("parallel", "parallel"),
        name="gather",
    )(clo, chi, x1b, pos, gate)


def _ffn_kernel(xe_ref, gs_ref, wg_ref, wu_ref, wd_ref, y_ref, acc_ref):
    f = pl.program_id(1)
    nf = pl.num_programs(1)
    bsz = xe_ref.shape[0]
    wg = wg_ref[0, 0].astype(BF16)
    wu = wu_ref[0, 0].astype(BF16)
    wd = wd_ref[0, 0].astype(BF16)

    for bi in range(bsz):
        x = xe_ref[bi, 0]
        a = _dot(x, wg)
        u = _dot(x, wu)
        hid = (a * _sigmoid(a) * u).astype(BF16)
        acc_ref[bi] = jnp.where(f == 0, 0.0, acc_ref[bi]) + _dot(hid, wd)

    @pl.when(f == nf - 1)
    def _():
        y_ref[:, 0] = (acc_ref[...] * gs_ref[:, 0]).astype(BF16)


def _ffn(layer, xe, gs, w_gate, w_up, w_down):
    bsz, ne, cap, d = xe.shape
    ff = w_gate.shape[3]
    tf = TF_FFN
    return pl.pallas_call(
        _ffn_kernel,
        grid=(ne, ff // tf),
        in_specs=[
            pl.BlockSpec((bsz, 1, cap, d), lambda e, f: (0, e, 0, 0)),
            pl.BlockSpec((bsz, 1, cap, 1), lambda e, f: (0, e, 0, 0)),
            pl.BlockSpec((1, 1, d, tf), lambda e, f: (layer, e, 0, f)),
            pl.BlockSpec((1, 1, d, tf), lambda e, f: (layer, e, 0, f)),
            pl.BlockSpec((1, 1, tf, d), lambda e, f: (layer, e, f, 0)),
        ],
        out_specs=pl.BlockSpec((bsz, 1, cap, d), lambda e, f: (0, e, 0, 0)),
        out_shape=jax.ShapeDtypeStruct((bsz, ne, cap, d), BF16),
        scratch_shapes=[pltpu.VMEM((bsz, cap, d), F32)],
        compiler_params=_params("parallel", "arbitrary"),
        name="ffn",
    )(xe, gs, w_gate, w_up, w_down)


def _combine_kernel(alpha, win, st_ref, xlo_ref, xhi_ref, y_ref, post_ref, x1_ref, g_ref, b_ref,
                    xo_ref, xbo_ref, moe_ref):
    b = pl.program_id(0)
    i = pl.program_id(1)
    ni = pl.num_programs(1)
    ne = y_ref.shape[1]
    tt = x1_ref.shape[1]
    lane_w = lax.broadcasted_iota(jnp.int32, (tt, win), 1).astype(F32)
    lane = lax.broadcasted_iota(jnp.int32, (tt, LANES), 1).astype(F32)
    moe = jnp.zeros((tt, x1_ref.shape[2]), F32)
    for e in range(ne):
        st = pl.multiple_of(st_ref[(b * ne + e) * ni + i], BF16_ROWS)
        rel = post_ref[0, :, e:e + 1] - st.astype(F32)
        onehot = jnp.where(rel == lane_w, 1.0, 0.0).astype(BF16)
        moe = moe + _dot(onehot, y_ref[0, e, pl.ds(st, win), :])
    moe_ref[...] = moe
    for e in range(ne):
        pcol = post_ref[0, :, e:e + 1]
        t = (b * ne + e) * ni + i
        past = pcol >= (st_ref[t] + win).astype(F32)

        def body(j, carry):
            match = jnp.logical_and(pcol - (j * LANES).astype(F32) == lane, past)
            onehot = jnp.where(match, 1.0, 0.0).astype(BF16)
            yk = y_ref[0, e, pl.ds(pl.multiple_of(j * LANES, LANES), LANES), :]
            moe_ref[...] += _dot(onehot, yk)
            return carry

        lax.fori_loop(xlo_ref[t], xhi_ref[t], body, 0)
    y = _layer_norm(alpha * x1_ref[0] + moe_ref[...], g_ref[...], b_ref[...])
    xo_ref[0] = y
    xbo_ref[0] = y.astype(BF16)


def _combine(alpha, win, st, xlo, xhi, y, pos_t, x1, g, b):
    bsz, seq, d = x1.shape
    ne, cap = y.shape[1], y.shape[2]
    tt = TT_COMB
    tok = pl.BlockSpec((1, tt, d), lambda bb, i, *_: (bb, i, 0))
    sel = pl.BlockSpec((1, tt, ne), lambda bb, i, *_: (bb, i, 0))
    vec = pl.BlockSpec((1, d), lambda bb, i, *_: (0, 0))
    y_spec = pl.BlockSpec((1, ne, cap, d), lambda bb, i, *_: (bb, 0, 0, 0), pipeline_mode=pl.Buffered(1))
    grid_spec = pltpu.PrefetchScalarGridSpec(
        num_scalar_prefetch=3,
        grid=(bsz, seq // tt),
        in_specs=[y_spec, sel, tok, vec, vec],
        out_specs=[tok, tok],
        scratch_shapes=[pltpu.VMEM((tt, d), F32)],
    )
    return pl.pallas_call(
        functools.partial(_combine_kernel, alpha, win),
        grid_spec=grid_spec,
        out_shape=[jax.ShapeDtypeStruct((bsz, seq, d), F32), jax.ShapeDtypeStruct((bsz, seq, d), BF16)],
        compiler_params=_params("parallel", "arbitrary"),
        name="combine",
    )(st, xlo, xhi, y, pos_t, x1, g.reshape(1, d), b.reshape(1, d))


def _rope_tables(seq):
    rows = seq // GRID_W
    row = jnp.repeat(jnp.arange(rows, dtype=jnp.int32), GRID_W).astype(F32)
    col = jnp.tile(jnp.arange(GRID_W, dtype=jnp.int32), rows).astype(F32)
    axis_dim = HEAD_DIM // 2
    freqs = 1.0 / (ROPE_THETA ** (jnp.arange(0, axis_dim, 2, dtype=F32) / axis_dim))
    ang = jnp.concatenate([row[:, None] * freqs[None], col[:, None] * freqs[None]], axis=-1)
    cos, sin = jnp.cos(ang), jnp.sin(ang)
    return jnp.concatenate([cos, cos], axis=-1), jnp.concatenate([-sin, sin], axis=-1)


def _deinterleave_heads(w, n_heads):
    lead = w.shape[:-1]
    w = w.reshape(lead + (n_heads, HEAD_DIM // 2, 2))
    return jnp.swapaxes(w, -1, -2).reshape(lead + (n_heads * HEAD_DIM,))


def _loop_tables(cum, cap, win):
    bsz, ne, nb = cum.shape
    cum_ext = jnp.concatenate([cum, jnp.full((bsz, ne, 1), cap, jnp.int32)], axis=-1)
    nj = cap // LANES
    slot0 = (jnp.arange(nj, dtype=jnp.int32) * LANES)[None, None, :, None]
    klo = jnp.sum((cum_ext[:, :, None, 1:] <= slot0).astype(jnp.int32), axis=-1)
    khi = jnp.sum((cum_ext[:, :, None, :nb] < slot0 + LANES).astype(jnp.int32), axis=-1)
    per = TC_GATHER // LANES
    clo = klo // per
    chi = (khi + per - 1) // per
    per = TT_COMB // LANES
    s0 = cum_ext[:, :, 0:nb:per]
    s1 = cum_ext[:, :, per::per]
    st = jnp.minimum((s0 // BF16_ROWS) * BF16_ROWS, cap - win)
    xlo = (st + win) // LANES
    xhi = jnp.where(s1 > st + win, (s1 - 1) // LANES + 1, 0)
    return clo.reshape(-1), chi.reshape(-1), st.reshape(-1), xlo.reshape(-1), xhi.reshape(-1)


def kernel(x, ln0_g, ln0_b, w_in, b_in, conv_dw, conv_dw_b, conv_ln_g, conv_ln_b, conv_pw_w, conv_pw_b, q_norm_g, k_norm_g, w_o, w_out, b_out, ln1_g, ln1_b, w_router, w_gate, w_up, w_down, ln2_g, ln2_b):
    bsz, seq, d = x.shape
    depth = w_in.shape[0]
    alpha = (2.0 * depth) ** 0.25
    cap = EC_FACTOR * seq // N_EXPERTS
    win = min(WIN_COMB, cap)
    m = bsz * seq
    nb = seq // LANES

    off_q = 2 * C_CONV
    off_k = off_q + Q_W
    off_v = off_k + KV_W
    off_gc = off_v + KV_W
    off_ga = off_gc + d

    cos, sin = _rope_tables(seq)
    scale = math.log2(math.e) / math.sqrt(HEAD_DIM)
    cq, sq = cos * scale, sin * scale
    gq = _deinterleave_heads(q_norm_g, 1)
    gk = _deinterleave_heads(k_norm_g, 1)

    w_in_b = w_in.astype(BF16)
    w_pw_b = conv_pw_w.astype(BF16)
    w_o_b = w_o.astype(BF16)
    w_out_b = w_out.astype(BF16)
    w_r = jnp.pad(w_router, ((0, 0), (0, 0), (0, LANES - N_EXPERTS)))
    w_r_hi = w_r.astype(BF16)
    w_r_lo = (w_r - w_r_hi.astype(F32)).astype(BF16)

    xf, xb = _ln0(x.reshape(m, d), ln0_g, ln0_b)
    for l in range(depth):
        wl, bl = w_in_b[l], b_in[l]
        u = _glu(xb, wl[:, :C_CONV], wl[:, C_CONV:off_q], bl[:C_CONV], bl[C_CONV:off_q])
        q, k, vt = _qkv(
            xb, _deinterleave_heads(wl[:, off_q:off_k], N_HEADS), _deinterleave_heads(wl[:, off_k:off_v], N_KV_HEADS),
            wl[:, off_v:off_gc], _deinterleave_heads(bl[off_q:off_k], N_HEADS),
            _deinterleave_heads(bl[off_k:off_v], N_KV_HEADS), bl[off_v:off_gc],
            gq[l], gk[l], cq, sq, cos, sin, seq)
        c = _conv(u.reshape(bsz, seq, C_CONV), conv_dw[l], conv_dw_b[l], conv_ln_g[l], conv_ln_b[l])
        a = _attention(q.reshape(bsz, seq, Q_W), k.reshape(bsz, seq, KV_AUG_W), vt)
        x1, x1b, logits = _merge(
            alpha, xb, xf, c.reshape(m, d), a.reshape(m, d),
            wl[:, off_gc:off_ga], wl[:, off_ga:], w_pw_b[l], w_o_b[l], w_out_b[l],
            bl[off_gc:off_ga], bl[off_ga:], conv_pw_b[l], b_out[l], ln1_g[l], ln1_b[l], w_r_hi[l], w_r_lo[l])
        logits_t = jnp.swapaxes(logits[:, :N_EXPERTS].reshape(bsz, seq, N_EXPERTS), 1, 2)
        pos, gate, cum = _route(logits_t.reshape(bsz, N_EXPERTS, nb, LANES), cap)
        clo, chi, st, xlo, xhi = _loop_tables(cum[..., 0].astype(jnp.int32), cap, win)
        xe, gs = _gather(clo, chi, x1b.reshape(bsz, seq, d), pos, gate, cap)
        y = _ffn(l, xe, gs, w_gate, w_up, w_down)
        pos_t = jnp.swapaxes(pos.reshape(bsz, N_EXPERTS, seq), 1, 2)
        xo, xbo = _combine(alpha, win, st, xlo, xhi, y, pos_t, x1.reshape(bsz, seq, d), ln2_g[l], ln2_b[l])
        xf, xb = xo.reshape(m, d), xbo.reshape(m, d)
    return xf.reshape(bsz, seq, d)
```

```python
import functools
import math

import jax
import jax.numpy as jnp
from jax import lax
from jax.experimental import pallas as pl
from jax.experimental.pallas import tpu as pltpu

F32 = jnp.float32
BF16 = jnp.bfloat16

D_MODEL = 1024
N_HEADS = 8
N_KV_HEADS = 2
HEAD_DIM = 128
GROUP = N_HEADS // N_KV_HEADS
GRID_W = 64
ROPE_THETA = 10000.0
C_CONV = D_MODEL
CONV_K = 31
CONV_HALO = 16
N_EXPERTS = 16
EC_FACTOR = 2
LN_EPS = 1e-5
RMS_EPS = 1e-6
Q_W = N_HEADS * HEAD_DIM
KV_W = N_KV_HEADS * HEAD_DIM
AUG_DIM = 2 * HEAD_DIM
KV_AUG_W = N_KV_HEADS * AUG_DIM
VT_ROWS = HEAD_DIM + 16
MAX_FIXED_SHIFT = 60.0
SHIFT_MARGIN = 1.01

LANES = 128
SUBLANES = 8
BF16_ROWS = 16
VMEM_LIMIT = 56 * 1024 * 1024

TM_LN = 512
TM_PROJ = 512
TM_QKV = 256
TS_CONV = 512
RC_CONV = 32
TQ_ATTN = 1024
TK_ATTN = 512
TM_MERGE = 256
TF_FFN = 512
TT_COMB = 512
WIN_COMB = 256
TC_GATHER = 256
NHEAD_GATHER = 5


def _params(*sem):
    return pltpu.CompilerParams(dimension_semantics=sem, vmem_limit_bytes=VMEM_LIMIT)


def _sigmoid(x):
    return 1.0 / (1.0 + jnp.exp(-x))


def _layer_norm(x, g, b):
    mu = jnp.mean(x, axis=-1, keepdims=True)
    xc = x - mu
    var = jnp.mean(xc * xc, axis=-1, keepdims=True)
    return xc * lax.rsqrt(var + LN_EPS) * g + b


def _dot(a, b):
    return jnp.dot(a, b, preferred_element_type=F32)


def _ln0_kernel(x_ref, g_ref, b_ref, xf_ref, xb_ref):
    y = _layer_norm(x_ref[...], g_ref[...], b_ref[...])
    xf_ref[...] = y
    xb_ref[...] = y.astype(BF16)


def _ln0(x2d, g, b):
    m, d = x2d.shape
    row = pl.BlockSpec((TM_LN, d), lambda i: (i, 0))
    vec = pl.BlockSpec((1, d), lambda i: (0, 0))
    return pl.pallas_call(
        _ln0_kernel,
        grid=(m // TM_LN,),
        in_specs=[row, vec, vec],
        out_specs=[row, row],
        out_shape=[jax.ShapeDtypeStruct((m, d), F32), jax.ShapeDtypeStruct((m, d), BF16)],
        compiler_params=_params("parallel"),
        name="ln0",
    )(x2d, g.reshape(1, d), b.reshape(1, d))


def _glu_kernel(h_ref, wv_ref, wg_ref, bv_ref, bg_ref, u_ref):
    h = h_ref[...]
    val = _dot(h, wv_ref[...]) + bv_ref[...]
    gate = _dot(h, wg_ref[...]) + bg_ref[...]
    u_ref[...] = val * _sigmoid(gate)


def _glu(hb, wv, wg, bv, bg):
    m, d = hb.shape
    c = wv.shape[1]
    row_in = pl.BlockSpec((TM_PROJ, d), lambda i: (i, 0))
    w = pl.BlockSpec((d, c), lambda i: (0, 0))
    vec = pl.BlockSpec((1, c), lambda i: (0, 0))
    return pl.pallas_call(
        _glu_kernel,
        grid=(m // TM_PROJ,),
        in_specs=[row_in, w, w, vec, vec],
        out_specs=pl.BlockSpec((TM_PROJ, c), lambda i: (i, 0)),
        out_shape=jax.ShapeDtypeStruct((m, c), F32),
        compiler_params=_params("parallel"),
        name="glu",
    )(hb, wv, wg, bv.reshape(1, c), bg.reshape(1, c))


def _norm_rope(x, g, cos, sin):
    ms = jnp.mean(x * x, axis=-1, keepdims=True)
    xn = x * lax.rsqrt(ms + RMS_EPS) * g
    return xn * cos + pltpu.roll(xn, HEAD_DIM // 2, 1) * sin


def _qkv_kernel(h_ref, wq_ref, wk_ref, wv_ref, bq_ref, bk_ref, bv_ref, gq_ref, gk_ref,
                cq_ref, sq_ref, ck_ref, sk_ref, q_ref, k_ref, vt_ref):
    h = h_ref[...]
    zq = _dot(h, wq_ref[...]) + bq_ref[...]
    zk = _dot(h, wk_ref[...]) + bk_ref[...]
    zv = _dot(h, wv_ref[...]) + bv_ref[...]
    cq, sq, ck, sk = cq_ref[...], sq_ref[...], ck_ref[...], sk_ref[...]
    gq, gk = gq_ref[...], gk_ref[...]
    for hd in range(N_HEADS):
        sl = slice(hd * HEAD_DIM, (hd + 1) * HEAD_DIM)
        q_ref[:, sl] = _norm_rope(zq[:, sl], gq, cq, sq).astype(BF16)
    tm = h.shape[0]
    one_col = jnp.where(lax.broadcasted_iota(jnp.int32, (tm, AUG_DIM - HEAD_DIM), 1) == 0, 1.0, 0.0).astype(BF16)
    one_row = jnp.where(lax.broadcasted_iota(jnp.int32, (VT_ROWS - HEAD_DIM, tm), 0) == 0, 1.0, 0.0).astype(BF16)
    for hd in range(N_KV_HEADS):
        sl = slice(hd * HEAD_DIM, (hd + 1) * HEAD_DIM)
        k_ref[:, hd * AUG_DIM:hd * AUG_DIM + HEAD_DIM] = _norm_rope(zk[:, sl], gk, ck, sk).astype(BF16)
        k_ref[:, hd * AUG_DIM + HEAD_DIM:(hd + 1) * AUG_DIM] = one_col
        vt_ref[0, hd * VT_ROWS:hd * VT_ROWS + HEAD_DIM, :] = zv[:, sl].T.astype(BF16)
        vt_ref[0, hd * VT_ROWS + HEAD_DIM:(hd + 1) * VT_ROWS, :] = one_row


def _qkv(hb, wq, wk, wv, bq, bk, bv, gq, gk, cq, sq, ck, sk, seq):
    m, d = hb.shape
    tm = TM_QKV
    per_seq = seq // tm
    row = lambda w: pl.BlockSpec((tm, w), lambda i: (i, 0))
    full = lambda a: pl.BlockSpec(a.shape, lambda i: (0, 0))
    tab = pl.BlockSpec((tm, HEAD_DIM), lambda i: (i % per_seq, 0))
    args = (hb, wq, wk, wv, bq.reshape(1, -1), bk.reshape(1, -1), bv.reshape(1, -1),
            gq.reshape(1, -1), gk.reshape(1, -1), cq, sq, ck, sk)
    in_specs = [row(d)] + [full(a) for a in args[1:9]] + [tab] * 4
    return pl.pallas_call(
        _qkv_kernel,
        grid=(m // tm,),
        in_specs=in_specs,
        out_specs=[row(Q_W), row(KV_AUG_W),
                   pl.BlockSpec((1, N_KV_HEADS * VT_ROWS, tm), lambda i: (i // per_seq, 0, i % per_seq))],
        out_shape=[jax.ShapeDtypeStruct((m, Q_W), BF16), jax.ShapeDtypeStruct((m, KV_AUG_W), BF16),
                   jax.ShapeDtypeStruct((m // seq, N_KV_HEADS * VT_ROWS, seq), BF16)],
        compiler_params=_params("parallel"),
        name="qkv",
    )(*args)


def _conv_kernel(up_ref, uc_ref, un_ref, w_ref, wb_ref, g_ref, b_ref, o_ref, ext_ref, sh_ref, wrep_ref):
    i = pl.program_id(1)
    n = pl.num_programs(1)
    ts = uc_ref.shape[1]
    ext_ref[0:CONV_HALO, :] = jnp.where(i > 0, up_ref[0], 0.0)
    ext_ref[CONV_HALO:CONV_HALO + ts, :] = uc_ref[0]
    ext_ref[CONV_HALO + ts:, :] = jnp.where(i < n - 1, un_ref[0], 0.0)
    span = sh_ref.shape[1]
    for s in range(SUBLANES):
        sh_ref[s] = ext_ref[s:s + span, :]
    c = uc_ref.shape[2]
    for k in range(CONV_K):
        wrep_ref[k] = jnp.broadcast_to(w_ref[k:k + 1, :], (SUBLANES, c))
    g, b = g_ref[...], b_ref[...]
    first = CONV_HALO - CONV_K // 2
    groups = RC_CONV // SUBLANES
    for r0 in range(0, ts, RC_CONV):
        acc = jnp.zeros((groups, SUBLANES, c), F32) + wb_ref[...]
        for k in range(CONV_K):
            a, s = divmod(first + k, SUBLANES)
            rows = sh_ref[s, r0 + a * SUBLANES:r0 + a * SUBLANES + RC_CONV, :]
            acc = acc + wrep_ref[k] * rows.reshape(groups, SUBLANES, c)
        y = _layer_norm(acc.reshape(RC_CONV, c), g, b)
        o_ref[0, r0:r0 + RC_CONV, :] = (y * _sigmoid(y)).astype(BF16)


def _conv(u, dw, dw_b, ln_g, ln_b):
    bsz, seq, c = u.shape
    ts = TS_CONV
    hb = ts // CONV_HALO
    last = seq // CONV_HALO - 1
    vec = pl.BlockSpec((1, c), lambda b, i: (0, 0))
    return pl.pallas_call(
        _conv_kernel,
        grid=(bsz, seq // ts),
        in_specs=[
            pl.BlockSpec((1, CONV_HALO, c), lambda b, i: (b, jnp.maximum(i * hb - 1, 0), 0)),
            pl.BlockSpec((1, ts, c), lambda b, i: (b, i, 0)),
            pl.BlockSpec((1, CONV_HALO, c), lambda b, i: (b, jnp.minimum((i + 1) * hb, last), 0)),
            pl.BlockSpec((CONV_K, c), lambda b, i: (0, 0)),
            vec, vec, vec,
        ],
        out_specs=pl.BlockSpec((1, ts, c), lambda b, i: (b, i, 0)),
        out_shape=jax.ShapeDtypeStruct((bsz, seq, c), BF16),
        scratch_shapes=[pltpu.VMEM((ts + 2 * CONV_HALO, c), F32),
                        pltpu.VMEM((SUBLANES, ts + 2 * CONV_HALO - SUBLANES, c), F32),
                        pltpu.VMEM((CONV_K, SUBLANES, c), F32)],
        compiler_params=_params("parallel", "parallel"),
        name="conv",
    )(u, u, u, dw, dw_b.reshape(1, c), ln_g.reshape(1, c), ln_b.reshape(1, c))


def _attn_kernel(shift_ref, q_ref, k_ref, vt_ref, o_ref):
    tq = q_ref.shape[1]
    nk = k_ref.shape[1] // TK_ATTN
    rows = GROUP * tq
    pad = AUG_DIM - HEAD_DIM

    q = jnp.concatenate([q_ref[0, :, g * HEAD_DIM:(g + 1) * HEAD_DIM] for g in range(GROUP)], axis=0)
    shift = shift_ref[...]
    first = lax.broadcasted_iota(jnp.int32, (rows, pad), 1) == 0
    nt = (((1,), (1,)), ((), ()))

    def chunk(c):
        start = pl.multiple_of(c * TK_ATTN, TK_ATTN)
        return k_ref[0, pl.ds(start, TK_ATTN), :], vt_ref[0, :, pl.ds(start, TK_ATTN)]

    def finish(acc):
        out = acc[:HEAD_DIM] / acc[HEAD_DIM:HEAD_DIM + 1]
        for g in range(GROUP):
            o_ref[0, :, g * HEAD_DIM:(g + 1) * HEAD_DIM] = out[:, g * tq:(g + 1) * tq].T.astype(BF16)

    def fixed_shift():
        qa = jnp.concatenate([q, jnp.where(first, -shift, 0.0).astype(BF16)], axis=1)

        def body(c, acc):
            k, vt = chunk(c)
            st = lax.dot_general(k, qa, nt, preferred_element_type=F32)
            return acc + _dot(vt, jnp.exp2(st).astype(BF16))

        finish(lax.fori_loop(0, nk, body, jnp.zeros((VT_ROWS, rows), F32), unroll=True))

    def running_max():
        qa = jnp.concatenate([q, jnp.zeros((rows, pad), BF16)], axis=1)

        def body(c, carry):
            m, acc = carry
            k, vt = chunk(c)
            st = lax.dot_general(k, qa, nt, preferred_element_type=F32)
            m_new = jnp.maximum(m, jnp.max(st, axis=0, keepdims=True))
            acc = jnp.exp2(m - m_new) * acc + _dot(vt, jnp.exp2(st - m_new).astype(BF16))
            return m_new, acc

        init = (jnp.full((1, rows), -jnp.inf, F32), jnp.zeros((VT_ROWS, rows), F32))
        finish(lax.fori_loop(0, nk, body, init)[1])

    lax.cond(jnp.max(shift) <= MAX_FIXED_SHIFT, fixed_shift, running_max)


def _attention(shift, q, k, vt):
    bsz, seq, _ = q.shape
    gw = GROUP * HEAD_DIM
    return pl.pallas_call(
        _attn_kernel,
        grid=(bsz, N_KV_HEADS, seq // TQ_ATTN),
        in_specs=[pl.BlockSpec((1, 1), lambda b, h, i: (0, 0)),
                  pl.BlockSpec((1, TQ_ATTN, gw), lambda b, h, i: (b, i, h)),
                  pl.BlockSpec((1, seq, AUG_DIM), lambda b, h, i: (b, 0, h)),
                  pl.BlockSpec((1, VT_ROWS, seq), lambda b, h, i: (b, h, 0))],
        out_specs=pl.BlockSpec((1, TQ_ATTN, gw), lambda b, h, i: (b, i, h)),
        out_shape=jax.ShapeDtypeStruct((bsz, seq, Q_W), BF16),
        compiler_params=_params("parallel", "parallel", "parallel"),
        name="attn",
    )(shift.reshape(1, 1), q, k, vt)


def _merge_kernel(alpha, h_ref, x_ref, c_ref, a_ref, wgc_ref, wga_ref, wpw_ref, wo_ref, wout_ref,
                  bgc_ref, bga_ref, bpw_ref, bout_ref, lg_ref, lb_ref, wrh_ref, wrl_ref,
                  x1_ref, x1b_ref, lgt_ref):
    h = h_ref[...]
    gc = _sigmoid(_dot(h, wgc_ref[...]) + bgc_ref[...])
    ga = _sigmoid(_dot(h, wga_ref[...]) + bga_ref[...])
    yc = _dot(c_ref[...], wpw_ref[...]) + bpw_ref[...]
    ya = _dot(a_ref[...], wo_ref[...])
    mix = _dot((gc * yc + ga * ya).astype(BF16), wout_ref[...]) + bout_ref[...]
    x1 = _layer_norm(alpha * x_ref[...] + mix, lg_ref[...], lb_ref[...])
    x1_ref[...] = x1
    hi = x1.astype(BF16)
    lo = (x1 - hi.astype(F32)).astype(BF16)
    x1b_ref[...] = hi
    lgt_ref[...] = _dot(hi, wrh_ref[...]) + _dot(lo, wrh_ref[...]) + _dot(hi, wrl_ref[...])


def _merge(alpha, hb, x, c, a, wgc, wga, wpw, wo, wout, bgc, bga, bpw, bout, lg, lb, wrh, wrl):
    m, d = x.shape
    tm = TM_MERGE
    row = pl.BlockSpec((tm, d), lambda i: (i, 0))
    w = pl.BlockSpec((d, d), lambda i: (0, 0), pipeline_mode=pl.Buffered(1))
    vec = pl.BlockSpec((1, d), lambda i: (0, 0))
    wr = pl.BlockSpec((d, LANES), lambda i: (0, 0))
    vecs = [v.reshape(1, d) for v in (bgc, bga, bpw, bout, lg, lb)]
    return pl.pallas_call(
        functools.partial(_merge_kernel, alpha),
        grid=(m // tm,),
        in_specs=[row] * 4 + [w] * 5 + [vec] * 6 + [wr, wr],
        out_specs=[row, row, pl.BlockSpec((tm, LANES), lambda i: (i, 0))],
        out_shape=[jax.ShapeDtypeStruct((m, d), F32), jax.ShapeDtypeStruct((m, d), BF16),
                   jax.ShapeDtypeStruct((m, LANES), F32)],
        compiler_params=_params("parallel"),
        name="merge",
    )(hb, x, c, a, wgc, wga, wpw, wo, wout, *vecs, wrh, wrl)


def _route_kernel(cap, lg_ref, pos_ref, gate_ref, cum_ref):
    lg = lg_ref[0]
    ne, nb, _ = lg.shape
    rows = ne * nb
    ex = jnp.exp(lg - jnp.max(lg, axis=0, keepdims=True))
    aff = ex / jnp.sum(ex, axis=0, keepdims=True)

    def count(mask):
        x = jnp.where(mask, 1.0, 0.0)
        return jnp.sum(jnp.sum(x, axis=1, keepdims=True), axis=2, keepdims=True)

    def bit_step(i, tb):
        cand = tb | jnp.left_shift(jnp.int32(1), 30 - i)
        ok = count(aff >= lax.bitcast_convert_type(cand, F32)) >= cap
        return jnp.where(ok, cand, tb)

    tb = lax.fori_loop(0, 31, bit_step, jnp.zeros((ne, 1, 1), jnp.int32))
    above = aff >= lax.bitcast_convert_type(tb + 1, F32)
    tied = jnp.logical_and(aff >= lax.bitcast_convert_type(tb, F32), jnp.logical_not(above))

    li = lax.broadcasted_iota(jnp.int32, (LANES, LANES), 0)
    lj = lax.broadcasted_iota(jnp.int32, (LANES, LANES), 1)
    upper = jnp.where(li <= lj, 1.0, 0.0).astype(BF16)
    ones = jnp.ones((LANES, LANES), BF16)
    ri = lax.broadcasted_iota(jnp.int32, (rows, rows), 0)
    rj = lax.broadcasted_iota(jnp.int32, (rows, rows), 1)
    shift = nb.bit_length() - 1
    assert 1 << shift == nb, "token blocks per sequence must be a power of two"
    same = jnp.right_shift(ri, shift) == jnp.right_shift(rj, shift)
    before = jnp.where(jnp.logical_and(same, rj < ri), 1.0, 0.0).astype(BF16)

    def prefix(mask):
        x = jnp.where(mask, 1.0, 0.0).reshape(rows, LANES)
        xb = x.astype(BF16)
        incl = _dot(xb, upper)
        tot = _dot(xb, ones)
        off = _dot(before, tot.astype(BF16))
        return (off + incl - x).reshape(ne, nb, LANES), off.reshape(ne, nb, LANES)

    need = cap - count(above)
    tie_rank, _ = prefix(tied)
    chosen = jnp.logical_or(above, jnp.logical_and(tied, tie_rank < need))
    pos, cum = prefix(chosen)
    pos_ref[0] = jnp.where(chosen, pos, -1.0)
    gate_ref[0] = jnp.where(chosen, aff, 0.0)
    cum_ref[0] = cum


def _route(logits_t, cap):
    bsz, ne, nb, _ = logits_t.shape
    blk = pl.BlockSpec((1, ne, nb, LANES), lambda b: (b, 0, 0, 0))
    shp = jax.ShapeDtypeStruct(logits_t.shape, F32)
    return pl.pallas_call(
        functools.partial(_route_kernel, cap),
        grid=(bsz,),
        in_specs=[blk],
        out_specs=[blk, blk, blk],
        out_shape=[shp, shp, shp],
        compiler_params=_params("parallel"),
        name="route",
    )(logits_t)


def _gather_kernel(clo_ref, chi_ref, x_ref, pos_ref, gate_ref, xe_ref, gs_ref, acc_ref, gacc_ref):
    b = pl.program_id(0)
    e = pl.program_id(1)
    ne = pl.num_programs(1)
    nj = xe_ref.shape[2] // LANES
    per = TC_GATHER // LANES
    nchunk = x_ref.shape[1] // TC_GATHER
    nhead = min(NHEAD_GATHER, nchunk)

    def span(c, n, j):
        rows = range(n * per)
        p = jnp.concatenate([pos_ref[0, 0, pl.ds(c * per + r, 1), :] for r in rows], axis=1)
        gt = jnp.concatenate([gate_ref[0, 0, pl.ds(c * per + r, 1), :] for r in rows], axis=1)
        slot = lax.broadcasted_iota(jnp.int32, (LANES, n * TC_GATHER), 0).astype(F32) + float(j * LANES)
        match = slot == p
        xk = x_ref[0, pl.ds(pl.multiple_of(c * TC_GATHER, TC_GATHER), n * TC_GATHER), :]
        return (_dot(jnp.where(match, 1.0, 0.0).astype(BF16), xk),
                jnp.sum(jnp.where(match, gt, 0.0), axis=1, keepdims=True))

    for j in range(nj):
        t = (b * ne + e) * nj + j
        c0 = jnp.minimum(clo_ref[t], nchunk - nhead)
        acc_ref[...], gacc_ref[...] = span(c0, nhead, j)

        def body(c, carry):
            d, g = span(c, 1, j)
            acc_ref[...] += d
            gacc_ref[...] += g
            return carry

        lax.fori_loop(c0 + nhead, chi_ref[t], body, 0)
        xe_ref[0, 0, j * LANES:(j + 1) * LANES, :] = acc_ref[...].astype(BF16)
        gs_ref[0, 0, j * LANES:(j + 1) * LANES, :] = gacc_ref[...]


def _gather(clo, chi, x1b, pos, gate, cap):
    bsz, seq, d = x1b.shape
    ne, nb = pos.shape[1], pos.shape[2]
    sel = pl.BlockSpec((1, 1, nb, LANES), lambda b, e, *_: (b, e, 0, 0))
    grid_spec = pltpu.PrefetchScalarGridSpec(
        num_scalar_prefetch=2,
        grid=(bsz, ne),
        in_specs=[pl.BlockSpec((1, seq, d), lambda b, e, *_: (b, 0, 0)), sel, sel],
        out_specs=[pl.BlockSpec((1, 1, cap, d), lambda b, e, *_: (b, e, 0, 0)),
                   pl.BlockSpec((1, 1, cap, 1), lambda b, e, *_: (b, e, 0, 0))],
        scratch_shapes=[pltpu.VMEM((LANES, d), F32), pltpu.VMEM((LANES, 1), F32)],
    )
    return pl.pallas_call(
        _gather_kernel,
        grid_spec=grid_spec,
        out_shape=[jax.ShapeDtypeStruct((bsz, ne, cap, d), BF16), jax.ShapeDtypeStruct((bsz, ne, cap, 1), F32)],
        compiler_params=_params("parallel", "parallel"),
        name="gather",
    )(clo, chi, x1b, pos, gate)


def _ffn_kernel(xe_ref, gs_ref, wg_ref, wu_ref, wd_ref, y_ref, acc_ref):
    f = pl.program_id(1)
    nf = pl.num_programs(1)
    bsz = xe_ref.shape[0]
    wg = wg_ref[0, 0].astype(BF16)
    wu = wu_ref[0, 0].astype(BF16)
    wd = wd_ref[0, 0].astype(BF16)

    for bi in range(bsz):
        x = xe_ref[bi, 0]
        a = _dot(x, wg)
        u = _dot(x, wu)
        hid = (a * _sigmoid(a) * u).astype(BF16)
        acc_ref[bi] = jnp.where(f == 0, 0.0, acc_ref[bi]) + _dot(hid, wd)

    @pl.when(f == nf - 1)
    def _():
        y_ref[:, 0] = (acc_ref[...] * gs_ref[:, 0]).astype(BF16)


def _ffn(layer, xe, gs, w_gate, w_up, w_down):
    bsz, ne, cap, d = xe.shape
    ff = w_gate.shape[3]
    tf = TF_FFN
    return pl.pallas_call(
        _ffn_kernel,
        grid=(ne, ff // tf),
        in_specs=[
            pl.BlockSpec((bsz, 1, cap, d), lambda e, f: (0, e, 0, 0)),
            pl.BlockSpec((bsz, 1, cap, 1), lambda e, f: (0, e, 0, 0)),
            pl.BlockSpec((1, 1, d, tf), lambda e, f: (layer, e, 0, f)),
            pl.BlockSpec((1, 1, d, tf), lambda e, f: (layer, e, 0, f)),
            pl.BlockSpec((1, 1, tf, d), lambda e, f: (layer, e, f, 0)),
        ],
        out_specs=pl.BlockSpec((bsz, 1, cap, d), lambda e, f: (0, e, 0, 0)),
        out_shape=jax.ShapeDtypeStruct((bsz, ne, cap, d), BF16),
        scratch_shapes=[pltpu.VMEM((bsz, cap, d), F32)],
        compiler_params=_params("parallel", "arbitrary"),
        name="ffn",
    )(xe, gs, w_gate, w_up, w_down)


def _combine_kernel(alpha, win, st_ref, xlo_ref, xhi_ref, y_ref, post_ref, x1_ref, g_ref, b_ref,
                    xo_ref, xbo_ref, moe_ref):
    b = pl.program_id(0)
    i = pl.program_id(1)
    ni = pl.num_programs(1)
    ne = y_ref.shape[1]
    tt = x1_ref.shape[1]
    lane_w = lax.broadcasted_iota(jnp.int32, (tt, win), 1).astype(F32)
    lane = lax.broadcasted_iota(jnp.int32, (tt, LANES), 1).astype(F32)
    moe = jnp.zeros((tt, x1_ref.shape[2]), F32)
    for e in range(ne):
        st = pl.multiple_of(st_ref[(b * ne + e) * ni + i], BF16_ROWS)
        rel = post_ref[0, :, e:e + 1] - st.astype(F32)
        onehot = jnp.where(rel == lane_w, 1.0, 0.0).astype(BF16)
        moe = moe + _dot(onehot, y_ref[0, e, pl.ds(st, win), :])
    moe_ref[...] = moe
    for e in range(ne):
        pcol = post_ref[0, :, e:e + 1]
        t = (b * ne + e) * ni + i
        past = pcol >= (st_ref[t] + win).astype(F32)

        def body(j, carry):
            match = jnp.logical_and(pcol - (j * LANES).astype(F32) == lane, past)
            onehot = jnp.where(match, 1.0, 0.0).astype(BF16)
            yk = y_ref[0, e, pl.ds(pl.multiple_of(j * LANES, LANES), LANES), :]
            moe_ref[...] += _dot(onehot, yk)
            return carry

        lax.fori_loop(xlo_ref[t], xhi_ref[t], body, 0)
    y = _layer_norm(alpha * x1_ref[0] + moe_ref[...], g_ref[...], b_ref[...])
    xo_ref[0] = y
    xbo_ref[0] = y.astype(BF16)


def _combine(alpha, win, st, xlo, xhi, y, pos_t, x1, g, b):
    bsz, seq, d = x1.shape
    ne, cap = y.shape[1], y.shape[2]
    tt = TT_COMB
    tok = pl.BlockSpec((1, tt, d), lambda bb, i, *_: (bb, i, 0))
    sel = pl.BlockSpec((1, tt, ne), lambda bb, i, *_: (bb, i, 0))
    vec = pl.BlockSpec((1, d), lambda bb, i, *_: (0, 0))
    y_spec = pl.BlockSpec((1, ne, cap, d), lambda bb, i, *_: (bb, 0, 0, 0))
    grid_spec = pltpu.PrefetchScalarGridSpec(
        num_scalar_prefetch=3,
        grid=(bsz, seq // tt),
        in_specs=[y_spec, sel, tok, vec, vec],
        out_specs=[tok, tok],
        scratch_shapes=[pltpu.VMEM((tt, d), F32)],
    )
    return pl.pallas_call(
        functools.partial(_combine_kernel, alpha, win),
        grid_spec=grid_spec,
        out_shape=[jax.ShapeDtypeStruct((bsz, seq, d), F32), jax.ShapeDtypeStruct((bsz, seq, d), BF16)],
        compiler_params=_params("parallel", "arbitrary"),
        name="combine",
    )(st, xlo, xhi, y, pos_t, x1, g.reshape(1, d), b.reshape(1, d))


def _rope_tables(seq):
    rows = seq // GRID_W
    row = jnp.repeat(jnp.arange(rows, dtype=jnp.int32), GRID_W).astype(F32)
    col = jnp.tile(jnp.arange(GRID_W, dtype=jnp.int32), rows).astype(F32)
    axis_dim = HEAD_DIM // 2
    freqs = 1.0 / (ROPE_THETA ** (jnp.arange(0, axis_dim, 2, dtype=F32) / axis_dim))
    ang = jnp.concatenate([row[:, None] * freqs[None], col[:, None] * freqs[None]], axis=-1)
    cos, sin = jnp.cos(ang), jnp.sin(ang)
    return jnp.concatenate([cos, cos], axis=-1), jnp.concatenate([-sin, sin], axis=-1)


def _deinterleave_heads(w, n_heads):
    lead = w.shape[:-1]
    w = w.reshape(lead + (n_heads, HEAD_DIM // 2, 2))
    return jnp.swapaxes(w, -1, -2).reshape(lead + (n_heads * HEAD_DIM,))


def _loop_tables(cum, cap, win):
    bsz, ne, nb = cum.shape
    cum_ext = jnp.concatenate([cum, jnp.full((bsz, ne, 1), cap, jnp.int32)], axis=-1)
    nj = cap // LANES
    slot0 = (jnp.arange(nj, dtype=jnp.int32) * LANES)[None, None, :, None]
    klo = jnp.sum((cum_ext[:, :, None, 1:] <= slot0).astype(jnp.int32), axis=-1)
    khi = jnp.sum((cum_ext[:, :, None, :nb] < slot0 + LANES).astype(jnp.int32), axis=-1)
    per = TC_GATHER // LANES
    clo = klo // per
    chi = (khi + per - 1) // per
    per = TT_COMB // LANES
    s0 = cum_ext[:, :, 0:nb:per]
    s1 = cum_ext[:, :, per::per]
    st = jnp.minimum((s0 // BF16_ROWS) * BF16_ROWS, cap - win)
    xlo = (st + win) // LANES
    xhi = jnp.where(s1 > st + win, (s1 - 1) // LANES + 1, 0)
    return clo.reshape(-1), chi.reshape(-1), st.reshape(-1), xlo.reshape(-1), xhi.reshape(-1)


def kernel(x, ln0_g, ln0_b, w_in, b_in, conv_dw, conv_dw_b, conv_ln_g, conv_ln_b, conv_pw_w, conv_pw_b, q_norm_g, k_norm_g, w_o, w_out, b_out, ln1_g, ln1_b, w_router, w_gate, w_up, w_down, ln2_g, ln2_b):
    bsz, seq, d = x.shape
    depth = w_in.shape[0]
    alpha = (2.0 * depth) ** 0.25
    cap = EC_FACTOR * seq // N_EXPERTS
    win = min(WIN_COMB, cap)
    m = bsz * seq
    nb = seq // LANES

    off_q = 2 * C_CONV
    off_k = off_q + Q_W
    off_v = off_k + KV_W
    off_gc = off_v + KV_W
    off_ga = off_gc + d

    cos, sin = _rope_tables(seq)
    scale = math.log2(math.e) / math.sqrt(HEAD_DIM)
    cq, sq = cos * scale, sin * scale
    gq = _deinterleave_heads(q_norm_g, 1)
    gk = _deinterleave_heads(k_norm_g, 1)

    w_in_b = w_in.astype(BF16)
    w_pw_b = conv_pw_w.astype(BF16)
    w_o_b = w_o.astype(BF16)
    w_out_b = w_out.astype(BF16)
    w_r = jnp.pad(w_router, ((0, 0), (0, 0), (0, LANES - N_EXPERTS)))
    w_r_hi = w_r.astype(BF16)
    w_r_lo = (w_r - w_r_hi.astype(F32)).astype(BF16)

    xf, xb = _ln0(x.reshape(m, d), ln0_g, ln0_b)
    for l in range(depth):
        wl, bl = w_in_b[l], b_in[l]
        u = _glu(xb, wl[:, :C_CONV], wl[:, C_CONV:off_q], bl[:C_CONV], bl[C_CONV:off_q])
        q, k, vt = _qkv(
            xb, _deinterleave_heads(wl[:, off_q:off_k], N_HEADS), _deinterleave_heads(wl[:, off_k:off_v], N_KV_HEADS),
            wl[:, off_v:off_gc], _deinterleave_heads(bl[off_q:off_k], N_HEADS),
            _deinterleave_heads(bl[off_k:off_v], N_KV_HEADS), bl[off_v:off_gc],
            gq[l], gk[l], cq, sq, cos, sin, seq)
        c = _conv(u.reshape(bsz, seq, C_CONV), conv_dw[l], conv_dw_b[l], conv_ln_g[l], conv_ln_b[l])
        bound = SHIFT_MARGIN * HEAD_DIM * scale * jnp.max(jnp.abs(q_norm_g[l])) * jnp.max(jnp.abs(k_norm_g[l]))
        a = _attention(bound, q.reshape(bsz, seq, Q_W), k.reshape(bsz, seq, KV_AUG_W), vt)
        x1, x1b, logits = _merge(
            alpha, xb, xf, c.reshape(m, d), a.reshape(m, d),
            wl[:, off_gc:off_ga], wl[:, off_ga:], w_pw_b[l], w_o_b[l], w_out_b[l],
            bl[off_gc:off_ga], bl[off_ga:], conv_pw_b[l], b_out[l], ln1_g[l], ln1_b[l], w_r_hi[l], w_r_lo[l])
        logits_t = jnp.swapaxes(logits[:, :N_EXPERTS].reshape(bsz, seq, N_EXPERTS), 1, 2)
        pos, gate, cum = _route(logits_t.reshape(bsz, N_EXPERTS, nb, LANES), cap)
        clo, chi, st, xlo, xhi = _loop_tables(cum[..., 0].astype(jnp.int32), cap, win)
        xe, gs = _gather(clo, chi, x1b.reshape(bsz, seq, d), pos, gate, cap)
        y = _ffn(l, xe, gs, w_gate, w_up, w_down)
        pos_t = jnp.swapaxes(pos.reshape(bsz, N_EXPERTS, seq), 1, 2)
        xo, xbo = _combine(alpha, win, st, xlo, xhi, y, pos_t, x1.reshape(bsz, seq, d), ln2_g[l], ln2_b[l])
        xf, xb = xo.reshape(m, d), xbo.reshape(m, d)
    return xf.reshape(bsz, seq, d)
```

```python
import functools
import math

import jax
import jax.numpy as jnp
from jax import lax
from jax.experimental import pallas as pl
from jax.experimental.pallas import tpu as pltpu

F32 = jnp.float32
BF16 = jnp.bfloat16

D_MODEL = 1024
N_HEADS = 8
N_KV_HEADS = 2
HEAD_DIM = 128
GROUP = N_HEADS // N_KV_HEADS
GRID_W = 64
ROPE_THETA = 10000.0
C_CONV = D_MODEL
CONV_K = 31
CONV_HALO = 16
N_EXPERTS = 16
EC_FACTOR = 2
LN_EPS = 1e-5
RMS_EPS = 1e-6
Q_W = N_HEADS * HEAD_DIM
KV_W = N_KV_HEADS * HEAD_DIM
AUG_DIM = 2 * HEAD_DIM
KV_AUG_W = N_KV_HEADS * AUG_DIM
VT_ROWS = HEAD_DIM + 16
MAX_FIXED_SHIFT = 60.0
SHIFT_MARGIN = 1.01

LANES = 128
SUBLANES = 8
BF16_ROWS = 16
VMEM_LIMIT = 56 * 1024 * 1024

TM_LN = 512
TM_PROJ = 512
TM_QKV = 256
TS_CONV = 512
RC_CONV = 32
TQ_ATTN = 1024
TK_ATTN = 512
TM_MERGE = 256
TF_FFN = 512
TT_COMB = 512
WIN_COMB = 256
TC_GATHER = 256
NHEAD_GATHER = 5


def _params(*sem):
    return pltpu.CompilerParams(dimension_semantics=sem, vmem_limit_bytes=VMEM_LIMIT)


def _sigmoid(x):
    return 1.0 / (1.0 + jnp.exp(-x))


def _layer_norm(x, g, b):
    mu = jnp.mean(x, axis=-1, keepdims=True)
    xc = x - mu
    var = jnp.mean(xc * xc, axis=-1, keepdims=True)
    return xc * lax.rsqrt(var + LN_EPS) * g + b


def _dot(a, b):
    return jnp.dot(a, b, preferred_element_type=F32)


def _ln0_kernel(x_ref, g_ref, b_ref, xf_ref, xb_ref):
    y = _layer_norm(x_ref[...], g_ref[...], b_ref[...])
    xf_ref[...] = y
    xb_ref[...] = y.astype(BF16)


def _ln0(x2d, g, b):
    m, d = x2d.shape
    row = pl.BlockSpec((TM_LN, d), lambda i: (i, 0))
    vec = pl.BlockSpec((1, d), lambda i: (0, 0))
    return pl.pallas_call(
        _ln0_kernel,
        grid=(m // TM_LN,),
        in_specs=[row, vec, vec],
        out_specs=[row, row],
        out_shape=[jax.ShapeDtypeStruct((m, d), F32), jax.ShapeDtypeStruct((m, d), BF16)],
        compiler_params=_params("parallel"),
        name="ln0",
    )(x2d, g.reshape(1, d), b.reshape(1, d))


def _glu_kernel(h_ref, wv_ref, wg_ref, bv_ref, bg_ref, u_ref):
    h = h_ref[...]
    val = _dot(h, wv_ref[...]) + bv_ref[...]
    gate = _dot(h, wg_ref[...]) + bg_ref[...]
    u_ref[...] = val * _sigmoid(gate)


def _glu(hb, wv, wg, bv, bg):
    m, d = hb.shape
    c = wv.shape[1]
    row_in = pl.BlockSpec((TM_PROJ, d), lambda i: (i, 0))
    w = pl.BlockSpec((d, c), lambda i: (0, 0))
    vec = pl.BlockSpec((1, c), lambda i: (0, 0))
    return pl.pallas_call(
        _glu_kernel,
        grid=(m // TM_PROJ,),
        in_specs=[row_in, w, w, vec, vec],
        out_specs=pl.BlockSpec((TM_PROJ, c), lambda i: (i, 0)),
        out_shape=jax.ShapeDtypeStruct((m, c), F32),
        compiler_params=_params("parallel"),
        name="glu",
    )(hb, wv, wg, bv.reshape(1, c), bg.reshape(1, c))


def _norm_rope(x, g, cos, sin):
    ms = jnp.mean(x * x, axis=-1, keepdims=True)
    xn = x * lax.rsqrt(ms + RMS_EPS) * g
    return xn * cos + pltpu.roll(xn, HEAD_DIM // 2, 1) * sin


def _qkv_kernel(h_ref, wq_ref, wk_ref, wv_ref, bq_ref, bk_ref, bv_ref, gq_ref, gk_ref,
                cq_ref, sq_ref, ck_ref, sk_ref, q_ref, k_ref, vt_ref):
    h = h_ref[...]
    zq = _dot(h, wq_ref[...]) + bq_ref[...]
    zk = _dot(h, wk_ref[...]) + bk_ref[...]
    zv = _dot(h, wv_ref[...]) + bv_ref[...]
    cq, sq, ck, sk = cq_ref[...], sq_ref[...], ck_ref[...], sk_ref[...]
    gq, gk = gq_ref[...], gk_ref[...]
    for hd in range(N_HEADS):
        sl = slice(hd * HEAD_DIM, (hd + 1) * HEAD_DIM)
        q_ref[:, sl] = _norm_rope(zq[:, sl], gq, cq, sq).astype(BF16)
    tm = h.shape[0]
    one_col = jnp.where(lax.broadcasted_iota(jnp.int32, (tm, AUG_DIM - HEAD_DIM), 1) == 0, 1.0, 0.0).astype(BF16)
    one_row = jnp.where(lax.broadcasted_iota(jnp.int32, (VT_ROWS - HEAD_DIM, tm), 0) == 0, 1.0, 0.0).astype(BF16)
    for hd in range(N_KV_HEADS):
        sl = slice(hd * HEAD_DIM, (hd + 1) * HEAD_DIM)
        k_ref[:, hd * AUG_DIM:hd * AUG_DIM + HEAD_DIM] = _norm_rope(zk[:, sl], gk, ck, sk).astype(BF16)
        k_ref[:, hd * AUG_DIM + HEAD_DIM:(hd + 1) * AUG_DIM] = one_col
        vt_ref[0, hd * VT_ROWS:hd * VT_ROWS + HEAD_DIM, :] = zv[:, sl].T.astype(BF16)
        vt_ref[0, hd * VT_ROWS + HEAD_DIM:(hd + 1) * VT_ROWS, :] = one_row


def _qkv(hb, wq, wk, wv, bq, bk, bv, gq, gk, cq, sq, ck, sk, seq):
    m, d = hb.shape
    tm = TM_QKV
    per_seq = seq // tm
    row = lambda w: pl.BlockSpec((tm, w), lambda i: (i, 0))
    full = lambda a: pl.BlockSpec(a.shape, lambda i: (0, 0))
    tab = pl.BlockSpec((tm, HEAD_DIM), lambda i: (i % per_seq, 0))
    args = (hb, wq, wk, wv, bq.reshape(1, -1), bk.reshape(1, -1), bv.reshape(1, -1),
            gq.reshape(1, -1), gk.reshape(1, -1), cq, sq, ck, sk)
    in_specs = [row(d)] + [full(a) for a in args[1:9]] + [tab] * 4
    return pl.pallas_call(
        _qkv_kernel,
        grid=(m // tm,),
        in_specs=in_specs,
        out_specs=[row(Q_W), row(KV_AUG_W),
                   pl.BlockSpec((1, N_KV_HEADS * VT_ROWS, tm), lambda i: (i // per_seq, 0, i % per_seq))],
        out_shape=[jax.ShapeDtypeStruct((m, Q_W), BF16), jax.ShapeDtypeStruct((m, KV_AUG_W), BF16),
                   jax.ShapeDtypeStruct((m // seq, N_KV_HEADS * VT_ROWS, seq), BF16)],
        compiler_params=_params("parallel"),
        name="qkv",
    )(*args)


def _conv_kernel(up_ref, uc_ref, un_ref, w_ref, wb_ref, g_ref, b_ref, o_ref, ext_ref, sh_ref, wrep_ref):
    i = pl.program_id(1)
    n = pl.num_programs(1)
    ts = uc_ref.shape[1]
    ext_ref[0:CONV_HALO, :] = jnp.where(i > 0, up_ref[0], 0.0)
    ext_ref[CONV_HALO:CONV_HALO + ts, :] = uc_ref[0]
    ext_ref[CONV_HALO + ts:, :] = jnp.where(i < n - 1, un_ref[0], 0.0)
    span = sh_ref.shape[1]
    for s in range(SUBLANES):
        sh_ref[s] = ext_ref[s:s + span, :]
    c = uc_ref.shape[2]
    for k in range(CONV_K):
        wrep_ref[k] = jnp.broadcast_to(w_ref[k:k + 1, :], (SUBLANES, c))
    g, b = g_ref[...], b_ref[...]
    first = CONV_HALO - CONV_K // 2
    groups = RC_CONV // SUBLANES
    for r0 in range(0, ts, RC_CONV):
        acc = jnp.zeros((groups, SUBLANES, c), F32) + wb_ref[...]
        for k in range(CONV_K):
            a, s = divmod(first + k, SUBLANES)
            rows = sh_ref[s, r0 + a * SUBLANES:r0 + a * SUBLANES + RC_CONV, :]
            acc = acc + wrep_ref[k] * rows.reshape(groups, SUBLANES, c)
        y = _layer_norm(acc.reshape(RC_CONV, c), g, b)
        o_ref[0, r0:r0 + RC_CONV, :] = (y * _sigmoid(y)).astype(BF16)


def _conv(u, dw, dw_b, ln_g, ln_b):
    bsz, seq, c = u.shape
    ts = TS_CONV
    hb = ts // CONV_HALO
    last = seq // CONV_HALO - 1
    vec = pl.BlockSpec((1, c), lambda b, i: (0, 0))
    return pl.pallas_call(
        _conv_kernel,
        grid=(bsz, seq // ts),
        in_specs=[
            pl.BlockSpec((1, CONV_HALO, c), lambda b, i: (b, jnp.maximum(i * hb - 1, 0), 0)),
            pl.BlockSpec((1, ts, c), lambda b, i: (b, i, 0)),
            pl.BlockSpec((1, CONV_HALO, c), lambda b, i: (b, jnp.minimum((i + 1) * hb, last), 0)),
            pl.BlockSpec((CONV_K, c), lambda b, i: (0, 0)),
            vec, vec, vec,
        ],
        out_specs=pl.BlockSpec((1, ts, c), lambda b, i: (b, i, 0)),
        out_shape=jax.ShapeDtypeStruct((bsz, seq, c), BF16),
        scratch_shapes=[pltpu.VMEM((ts + 2 * CONV_HALO, c), F32),
                        pltpu.VMEM((SUBLANES, ts + 2 * CONV_HALO - SUBLANES, c), F32),
                        pltpu.VMEM((CONV_K, SUBLANES, c), F32)],
        compiler_params=_params("parallel", "parallel"),
        name="conv",
    )(u, u, u, dw, dw_b.reshape(1, c), ln_g.reshape(1, c), ln_b.reshape(1, c))


def _attn_kernel(shift_ref, q_ref, k_ref, vt_ref, o_ref):
    tq = q_ref.shape[1]
    nk = k_ref.shape[1] // TK_ATTN
    rows = GROUP * tq
    pad = AUG_DIM - HEAD_DIM

    q = jnp.concatenate([q_ref[0, :, g * HEAD_DIM:(g + 1) * HEAD_DIM] for g in range(GROUP)], axis=0)
    shift = shift_ref[...]
    first = lax.broadcasted_iota(jnp.int32, (rows, pad), 1) == 0
    nt = (((1,), (1,)), ((), ()))

    def chunk(c):
        start = pl.multiple_of(c * TK_ATTN, TK_ATTN)
        return k_ref[0, pl.ds(start, TK_ATTN), :], vt_ref[0, :, pl.ds(start, TK_ATTN)]

    def finish(acc):
        out = acc[:HEAD_DIM] / acc[HEAD_DIM:HEAD_DIM + 1]
        for g in range(GROUP):
            o_ref[0, :, g * HEAD_DIM:(g + 1) * HEAD_DIM] = out[:, g * tq:(g + 1) * tq].T.astype(BF16)

    def fixed_shift():
        qa = jnp.concatenate([q, jnp.where(first, -shift, 0.0).astype(BF16)], axis=1)

        def body(c, acc):
            k, vt = chunk(c)
            st = lax.dot_general(k, qa, nt, preferred_element_type=F32)
            return acc + _dot(vt, jnp.exp2(st).astype(BF16))

        finish(lax.fori_loop(0, nk, body, jnp.zeros((VT_ROWS, rows), F32), unroll=True))

    def running_max():
        qa = jnp.concatenate([q, jnp.zeros((rows, pad), BF16)], axis=1)

        def body(c, carry):
            m, acc = carry
            k, vt = chunk(c)
            st = lax.dot_general(k, qa, nt, preferred_element_type=F32)
            m_new = jnp.maximum(m, jnp.max(st, axis=0, keepdims=True))
            acc = jnp.exp2(m - m_new) * acc + _dot(vt, jnp.exp2(st - m_new).astype(BF16))
            return m_new, acc

        init = (jnp.full((1, rows), -jnp.inf, F32), jnp.zeros((VT_ROWS, rows), F32))
        finish(lax.fori_loop(0, nk, body, init)[1])

    lax.cond(jnp.max(shift) <= MAX_FIXED_SHIFT, fixed_shift, running_max)


def _attention(shift, q, k, vt):
    bsz, seq, _ = q.shape
    gw = GROUP * HEAD_DIM
    return pl.pallas_call(
        _attn_kernel,
        grid=(bsz, N_KV_HEADS, seq // TQ_ATTN),
        in_specs=[pl.BlockSpec((1, 1), lambda b, h, i: (0, 0)),
                  pl.BlockSpec((1, TQ_ATTN, gw), lambda b, h, i: (b, i, h)),
                  pl.BlockSpec((1, seq, AUG_DIM), lambda b, h, i: (b, 0, h)),
                  pl.BlockSpec((1, VT_ROWS, seq), lambda b, h, i: (b, h, 0))],
        out_specs=pl.BlockSpec((1, TQ_ATTN, gw), lambda b, h, i: (b, i, h)),
        out_shape=jax.ShapeDtypeStruct((bsz, seq, Q_W), BF16),
        compiler_params=_params("parallel", "parallel", "parallel"),
        name="attn",
    )(shift.reshape(1, 1), q, k, vt)


def _merge_kernel(alpha, h_ref, x_ref, c_ref, a_ref, wgc_ref, wga_ref, wpw_ref, wo_ref, wout_ref,
                  bgc_ref, bga_ref, bpw_ref, bout_ref, lg_ref, lb_ref, wr_ref,
                  x1_ref, x1b_ref, lgt_ref):
    h = h_ref[...]
    gc = _sigmoid(_dot(h, wgc_ref[...]) + bgc_ref[...])
    ga = _sigmoid(_dot(h, wga_ref[...]) + bga_ref[...])
    yc = _dot(c_ref[...], wpw_ref[...]) + bpw_ref[...]
    ya = _dot(a_ref[...], wo_ref[...])
    mix = _dot((gc * yc + ga * ya).astype(BF16), wout_ref[...]) + bout_ref[...]
    x1 = _layer_norm(alpha * x_ref[...] + mix, lg_ref[...], lb_ref[...])
    x1_ref[...] = x1
    hi = x1.astype(BF16)
    lo = (x1 - hi.astype(F32)).astype(BF16)
    x1b_ref[...] = hi
    tm = hi.shape[0]
    prod = _dot(jnp.concatenate([hi, lo], axis=0), wr_ref[...])
    top = prod[:tm]
    lgt_ref[...] = top + pltpu.roll(top, LANES - N_EXPERTS, 1) + prod[tm:]


def _merge(alpha, hb, x, c, a, wgc, wga, wpw, wo, wout, bgc, bga, bpw, bout, lg, lb, wr_packed):
    m, d = x.shape
    tm = TM_MERGE
    row = pl.BlockSpec((tm, d), lambda i: (i, 0))
    w = pl.BlockSpec((d, d), lambda i: (0, 0), pipeline_mode=pl.Buffered(1))
    vec = pl.BlockSpec((1, d), lambda i: (0, 0))
    wr = pl.BlockSpec((d, LANES), lambda i: (0, 0))
    vecs = [v.reshape(1, d) for v in (bgc, bga, bpw, bout, lg, lb)]
    return pl.pallas_call(
        functools.partial(_merge_kernel, alpha),
        grid=(m // tm,),
        in_specs=[row] * 4 + [w] * 5 + [vec] * 6 + [wr],
        out_specs=[row, row, pl.BlockSpec((tm, LANES), lambda i: (i, 0))],
        out_shape=[jax.ShapeDtypeStruct((m, d), F32), jax.ShapeDtypeStruct((m, d), BF16),
                   jax.ShapeDtypeStruct((m, LANES), F32)],
        compiler_params=_params("parallel"),
        name="merge",
    )(hb, x, c, a, wgc, wga, wpw, wo, wout, *vecs, wr_packed)


def _route_kernel(cap, lg_ref, pos_ref, gate_ref, cum_ref):
    lg = lg_ref[0]
    ne, nb, _ = lg.shape
    rows = ne * nb
    ex = jnp.exp(lg - jnp.max(lg, axis=0, keepdims=True))
    aff = ex / jnp.sum(ex, axis=0, keepdims=True)

    def count(mask):
        x = jnp.where(mask, 1.0, 0.0)
        return jnp.sum(jnp.sum(x, axis=1, keepdims=True), axis=2, keepdims=True)

    def bit_step(i, tb):
        cand = tb | jnp.left_shift(jnp.int32(1), 30 - i)
        ok = count(aff >= lax.bitcast_convert_type(cand, F32)) >= cap
        return jnp.where(ok, cand, tb)

    tb = lax.fori_loop(0, 31, bit_step, jnp.zeros((ne, 1, 1), jnp.int32))
    above = aff >= lax.bitcast_convert_type(tb + 1, F32)
    tied = jnp.logical_and(aff >= lax.bitcast_convert_type(tb, F32), jnp.logical_not(above))

    li = lax.broadcasted_iota(jnp.int32, (LANES, LANES), 0)
    lj = lax.broadcasted_iota(jnp.int32, (LANES, LANES), 1)
    upper = jnp.where(li <= lj, 1.0, 0.0).astype(BF16)
    ones = jnp.ones((LANES, LANES), BF16)
    ri = lax.broadcasted_iota(jnp.int32, (rows, rows), 0)
    rj = lax.broadcasted_iota(jnp.int32, (rows, rows), 1)
    shift = nb.bit_length() - 1
    assert 1 << shift == nb, "token blocks per sequence must be a power of two"
    same = jnp.right_shift(ri, shift) == jnp.right_shift(rj, shift)
    before = jnp.where(jnp.logical_and(same, rj < ri), 1.0, 0.0).astype(BF16)

    def prefix(mask):
        x = jnp.where(mask, 1.0, 0.0).reshape(rows, LANES)
        xb = x.astype(BF16)
        incl = _dot(xb, upper)
        tot = _dot(xb, ones)
        off = _dot(before, tot.astype(BF16))
        return (off + incl - x).reshape(ne, nb, LANES), off.reshape(ne, nb, LANES)

    need = cap - count(above)
    tie_rank, _ = prefix(tied)
    chosen = jnp.logical_or(above, jnp.logical_and(tied, tie_rank < need))
    pos, cum = prefix(chosen)
    pos_ref[0] = jnp.where(chosen, pos, -1.0)
    gate_ref[0] = jnp.where(chosen, aff, 0.0)
    cum_ref[0] = cum


def _route(logits_t, cap):
    bsz, ne, nb, _ = logits_t.shape
    blk = pl.BlockSpec((1, ne, nb, LANES), lambda b: (b, 0, 0, 0))
    shp = jax.ShapeDtypeStruct(logits_t.shape, F32)
    return pl.pallas_call(
        functools.partial(_route_kernel, cap),
        grid=(bsz,),
        in_specs=[blk],
        out_specs=[blk, blk, blk],
        out_shape=[shp, shp, shp],
        compiler_params=_params("parallel"),
        name="route",
    )(logits_t)


def _gather_kernel(clo_ref, chi_ref, x_ref, pos_ref, gate_ref, xe_ref, gs_ref, acc_ref, gacc_ref):
    b = pl.program_id(0)
    e = pl.program_id(1)
    ne = pl.num_programs(1)
    nj = xe_ref.shape[2] // LANES
    per = TC_GATHER // LANES
    nchunk = x_ref.shape[1] // TC_GATHER
    nhead = min(NHEAD_GATHER, nchunk)

    def span(c, n, j):
        rows = range(n * per)
        p = jnp.concatenate([pos_ref[0, 0, pl.ds(c * per + r, 1), :] for r in rows], axis=1)
        gt = jnp.concatenate([gate_ref[0, 0, pl.ds(c * per + r, 1), :] for r in rows], axis=1)
        slot = lax.broadcasted_iota(jnp.int32, (LANES, n * TC_GATHER), 0).astype(F32) + float(j * LANES)
        match = slot == p
        xk = x_ref[0, pl.ds(pl.multiple_of(c * TC_GATHER, TC_GATHER), n * TC_GATHER), :]
        return (_dot(jnp.where(match, 1.0, 0.0).astype(BF16), xk),
                jnp.sum(jnp.where(match, gt, 0.0), axis=1, keepdims=True))

    for j in range(nj):
        t = (b * ne + e) * nj + j
        c0 = jnp.minimum(clo_ref[t], nchunk - nhead)
        acc_ref[...], gacc_ref[...] = span(c0, nhead, j)

        def body(c, carry):
            d, g = span(c, 1, j)
            acc_ref[...] += d
            gacc_ref[...] += g
            return carry

        lax.fori_loop(c0 + nhead, chi_ref[t], body, 0)
        xe_ref[0, 0, j * LANES:(j + 1) * LANES, :] = acc_ref[...].astype(BF16)
        gs_ref[0, 0, j * LANES:(j + 1) * LANES, :] = gacc_ref[...]


def _gather(clo, chi, x1b, pos, gate, cap):
    bsz, seq, d = x1b.shape
    ne, nb = pos.shape[1], pos.shape[2]
    sel = pl.BlockSpec((1, 1, nb, LANES), lambda b, e, *_: (b, e, 0, 0))
    grid_spec = pltpu.PrefetchScalarGridSpec(
        num_scalar_prefetch=2,
        grid=(bsz, ne),
        in_specs=[pl.BlockSpec((1, seq, d), lambda b, e, *_: (b, 0, 0)), sel, sel],
        out_specs=[pl.BlockSpec((1, 1, cap, d), lambda b, e, *_: (b, e, 0, 0)),
                   pl.BlockSpec((1, 1, cap, 1), lambda b, e, *_: (b, e, 0, 0))],
        scratch_shapes=[pltpu.VMEM((LANES, d), F32), pltpu.VMEM((LANES, 1), F32)],
    )
    return pl.pallas_call(
        _gather_kernel,
        grid_spec=grid_spec,
        out_shape=[jax.ShapeDtypeStruct((bsz, ne, cap, d), BF16), jax.ShapeDtypeStruct((bsz, ne, cap, 1), F32)],
        compiler_params=_params("parallel", "parallel"),
        name="gather",
    )(clo, chi, x1b, pos, gate)


def _ffn_kernel(xe_ref, gs_ref, wg_ref, wu_ref, wd_ref, y_ref, acc_ref):
    f = pl.program_id(1)
    nf = pl.num_programs(1)
    bsz = xe_ref.shape[0]
    wg = wg_ref[0, 0].astype(BF16)
    wu = wu_ref[0, 0].astype(BF16)
    wd = wd_ref[0, 0].astype(BF16)

    for bi in range(bsz):
        x = xe_ref[bi, 0]
        a = _dot(x, wg)
        u = _dot(x, wu)
        hid = (a * _sigmoid(a) * u).astype(BF16)
        acc_ref[bi] = jnp.where(f == 0, 0.0, acc_ref[bi]) + _dot(hid, wd)

    @pl.when(f == nf - 1)
    def _():
        y_ref[:, 0] = (acc_ref[...] * gs_ref[:, 0]).astype(BF16)


def _ffn(layer, xe, gs, w_gate, w_up, w_down):
    bsz, ne, cap, d = xe.shape
    ff = w_gate.shape[3]
    tf = TF_FFN
    return pl.pallas_call(
        _ffn_kernel,
        grid=(ne, ff // tf),
        in_specs=[
            pl.BlockSpec((bsz, 1, cap, d), lambda e, f: (0, e, 0, 0)),
            pl.BlockSpec((bsz, 1, cap, 1), lambda e, f: (0, e, 0, 0)),
            pl.BlockSpec((1, 1, d, tf), lambda e, f: (layer, e, 0, f)),
            pl.BlockSpec((1, 1, d, tf), lambda e, f: (layer, e, 0, f)),
            pl.BlockSpec((1, 1, tf, d), lambda e, f: (layer, e, f, 0)),
        ],
        out_specs=pl.BlockSpec((bsz, 1, cap, d), lambda e, f: (0, e, 0, 0)),
        out_shape=jax.ShapeDtypeStruct((bsz, ne, cap, d), BF16),
        scratch_shapes=[pltpu.VMEM((bsz, cap, d), F32)],
        compiler_params=_params("parallel", "arbitrary"),
        name="ffn",
    )(xe, gs, w_gate, w_up, w_down)


def _combine_kernel(alpha, win, st_ref, xlo_ref, xhi_ref, y_ref, post_ref, x1_ref, g_ref, b_ref,
                    xo_ref, xbo_ref, moe_ref):
    b = pl.program_id(0)
    i = pl.program_id(1)
    ni = pl.num_programs(1)
    ne = y_ref.shape[1]
    tt = x1_ref.shape[1]
    lane_w = lax.broadcasted_iota(jnp.int32, (tt, win), 1).astype(F32)
    lane = lax.broadcasted_iota(jnp.int32, (tt, LANES), 1).astype(F32)
    moe = jnp.zeros((tt, x1_ref.shape[2]), F32)
    for e in range(ne):
        st = pl.multiple_of(st_ref[(b * ne + e) * ni + i], BF16_ROWS)
        rel = post_ref[0, :, e:e + 1] - st.astype(F32)
        onehot = jnp.where(rel == lane_w, 1.0, 0.0).astype(BF16)
        moe = moe + _dot(onehot, y_ref[0, e, pl.ds(st, win), :])
    moe_ref[...] = moe
    for e in range(ne):
        pcol = post_ref[0, :, e:e + 1]
        t = (b * ne + e) * ni + i
        past = pcol >= (st_ref[t] + win).astype(F32)

        def body(j, carry):
            match = jnp.logical_and(pcol - (j * LANES).astype(F32) == lane, past)
            onehot = jnp.where(match, 1.0, 0.0).astype(BF16)
            yk = y_ref[0, e, pl.ds(pl.multiple_of(j * LANES, LANES), LANES), :]
            moe_ref[...] += _dot(onehot, yk)
            return carry

        lax.fori_loop(xlo_ref[t], xhi_ref[t], body, 0)
    y = _layer_norm(alpha * x1_ref[0] + moe_ref[...], g_ref[...], b_ref[...])
    xo_ref[0] = y
    xbo_ref[0] = y.astype(BF16)


def _combine(alpha, win, st, xlo, xhi, y, pos_t, x1, g, b):
    bsz, seq, d = x1.shape
    ne, cap = y.shape[1], y.shape[2]
    tt = TT_COMB
    tok = pl.BlockSpec((1, tt, d), lambda bb, i, *_: (bb, i, 0))
    sel = pl.BlockSpec((1, tt, ne), lambda bb, i, *_: (bb, i, 0))
    vec = pl.BlockSpec((1, d), lambda bb, i, *_: (0, 0))
    y_spec = pl.BlockSpec((1, ne, cap, d), lambda bb, i, *_: (bb, 0, 0, 0))
    grid_spec = pltpu.PrefetchScalarGridSpec(
        num_scalar_prefetch=3,
        grid=(bsz, seq // tt),
        in_specs=[y_spec, sel, tok, vec, vec],
        out_specs=[tok, tok],
        scratch_shapes=[pltpu.VMEM((tt, d), F32)],
    )
    return pl.pallas_call(
        functools.partial(_combine_kernel, alpha, win),
        grid_spec=grid_spec,
        out_shape=[jax.ShapeDtypeStruct((bsz, seq, d), F32), jax.ShapeDtypeStruct((bsz, seq, d), BF16)],
        compiler_params=_params("parallel", "arbitrary"),
        name="combine",
    )(st, xlo, xhi, y, pos_t, x1, g.reshape(1, d), b.reshape(1, d))


def _rope_tables(seq):
    rows = seq // GRID_W
    row = jnp.repeat(jnp.arange(rows, dtype=jnp.int32), GRID_W).astype(F32)
    col = jnp.tile(jnp.arange(GRID_W, dtype=jnp.int32), rows).astype(F32)
    axis_dim = HEAD_DIM // 2
    freqs = 1.0 / (ROPE_THETA ** (jnp.arange(0, axis_dim, 2, dtype=F32) / axis_dim))
    ang = jnp.concatenate([row[:, None] * freqs[None], col[:, None] * freqs[None]], axis=-1)
    cos, sin = jnp.cos(ang), jnp.sin(ang)
    return jnp.concatenate([cos, cos], axis=-1), jnp.concatenate([-sin, sin], axis=-1)


def _deinterleave_heads(w, n_heads):
    lead = w.shape[:-1]
    w = w.reshape(lead + (n_heads, HEAD_DIM // 2, 2))
    return jnp.swapaxes(w, -1, -2).reshape(lead + (n_heads * HEAD_DIM,))


def _loop_tables(cum, cap, win):
    bsz, ne, nb = cum.shape
    cum_ext = jnp.concatenate([cum, jnp.full((bsz, ne, 1), cap, jnp.int32)], axis=-1)
    nj = cap // LANES
    slot0 = (jnp.arange(nj, dtype=jnp.int32) * LANES)[None, None, :, None]
    klo = jnp.sum((cum_ext[:, :, None, 1:] <= slot0).astype(jnp.int32), axis=-1)
    khi = jnp.sum((cum_ext[:, :, None, :nb] < slot0 + LANES).astype(jnp.int32), axis=-1)
    per = TC_GATHER // LANES
    clo = klo // per
    chi = (khi + per - 1) // per
    per = TT_COMB // LANES
    s0 = cum_ext[:, :, 0:nb:per]
    s1 = cum_ext[:, :, per::per]
    st = jnp.minimum((s0 // BF16_ROWS) * BF16_ROWS, cap - win)
    xlo = (st + win) // LANES
    xhi = jnp.where(s1 > st + win, (s1 - 1) // LANES + 1, 0)
    return clo.reshape(-1), chi.reshape(-1), st.reshape(-1), xlo.reshape(-1), xhi.reshape(-1)


def kernel(x, ln0_g, ln0_b, w_in, b_in, conv_dw, conv_dw_b, conv_ln_g, conv_ln_b, conv_pw_w, conv_pw_b, q_norm_g, k_norm_g, w_o, w_out, b_out, ln1_g, ln1_b, w_router, w_gate, w_up, w_down, ln2_g, ln2_b):
    bsz, seq, d = x.shape
    depth = w_in.shape[0]
    alpha = (2.0 * depth) ** 0.25
    cap = EC_FACTOR * seq // N_EXPERTS
    win = min(WIN_COMB, cap)
    m = bsz * seq
    nb = seq // LANES

    off_q = 2 * C_CONV
    off_k = off_q + Q_W
    off_v = off_k + KV_W
    off_gc = off_v + KV_W
    off_ga = off_gc + d

    cos, sin = _rope_tables(seq)
    scale = math.log2(math.e) / math.sqrt(HEAD_DIM)
    cq, sq = cos * scale, sin * scale
    gq = _deinterleave_heads(q_norm_g, 1)
    gk = _deinterleave_heads(k_norm_g, 1)

    w_in_b = w_in.astype(BF16)
    w_pw_b = conv_pw_w.astype(BF16)
    w_o_b = w_o.astype(BF16)
    w_out_b = w_out.astype(BF16)
    w_r_hi = w_router.astype(BF16)
    w_r_lo = (w_router - w_r_hi.astype(F32)).astype(BF16)
    w_r_packed = jnp.pad(jnp.concatenate([w_r_hi, w_r_lo], axis=-1), ((0, 0), (0, 0), (0, LANES - 2 * N_EXPERTS)))

    xf, xb = _ln0(x.reshape(m, d), ln0_g, ln0_b)
    for l in range(depth):
        wl, bl = w_in_b[l], b_in[l]
        u = _glu(xb, wl[:, :C_CONV], wl[:, C_CONV:off_q], bl[:C_CONV], bl[C_CONV:off_q])
        q, k, vt = _qkv(
            xb, _deinterleave_heads(wl[:, off_q:off_k], N_HEADS), _deinterleave_heads(wl[:, off_k:off_v], N_KV_HEADS),
            wl[:, off_v:off_gc], _deinterleave_heads(bl[off_q:off_k], N_HEADS),
            _deinterleave_heads(bl[off_k:off_v], N_KV_HEADS), bl[off_v:off_gc],
            gq[l], gk[l], cq, sq, cos, sin, seq)
        c = _conv(u.reshape(bsz, seq, C_CONV), conv_dw[l], conv_dw_b[l], conv_ln_g[l], conv_ln_b[l])
        bound = SHIFT_MARGIN * HEAD_DIM * scale * jnp.max(jnp.abs(q_norm_g[l])) * jnp.max(jnp.abs(k_norm_g[l]))
        a = _attention(bound, q.reshape(bsz, seq, Q_W), k.reshape(bsz, seq, KV_AUG_W), vt)
        x1, x1b, logits = _merge(
            alpha, xb, xf, c.reshape(m, d), a.reshape(m, d),
            wl[:, off_gc:off_ga], wl[:, off_ga:], w_pw_b[l], w_o_b[l], w_out_b[l],
            bl[off_gc:off_ga], bl[off_ga:], conv_pw_b[l], b_out[l], ln1_g[l], ln1_b[l], w_r_packed[l])
        logits_t = jnp.swapaxes(logits[:, :N_EXPERTS].reshape(bsz, seq, N_EXPERTS), 1, 2)
        pos, gate, cum = _route(logits_t.reshape(bsz, N_EXPERTS, nb, LANES), cap)
        clo, chi, st, xlo, xhi = _loop_tables(cum[..., 0].astype(jnp.int32), cap, win)
        xe, gs = _gather(clo, chi, x1b.reshape(bsz, seq, d), pos, gate, cap)
        y = _ffn(l, xe, gs, w_gate, w_up, w_down)
        pos_t = jnp.swapaxes(pos.reshape(bsz, N_EXPERTS, seq), 1, 2)
        xo, xbo = _combine(alpha, win, st, xlo, xhi, y, pos_t, x1.reshape(bsz, seq, d), ln2_g[l], ln2_b[l])
        xf, xb = xo.reshape(m, d), xbo.reshape(m, d)
    return xf.reshape(bsz, seq, d)
```
